```python
import jax, jax.numpy as jnp
from jax import lax
import numpy as np

D_MODEL = 1024
BATCH = 2
SEQ = 8192
DEPTH = 1
DEC_BATCH = 128
DEC_SEQ = 8
PAST_LEN = 2048
PAGE_SIZE = 128

D_MIX = D_MODEL
HG_HEADS = 8
HG_DK = 64
HG_DV = 64
HG_QK = HG_HEADS * HG_DK
HG_WIDTH = HG_HEADS * HG_DV
HG_CHUNK = 64
SB_HEADS = 8
SB_DH = 64
SB_WIDTH = SB_HEADS * SB_DH
SB_QBLOCK = 128
SB_SCALE = SB_DH ** -0.5
SB_BIAS_INIT = -7.0
N_MEM = 256
CA_HEADS = 4
CA_DH = D_MODEL // CA_HEADS
D_FF = 2816
CONV_W = 3
RMS_EPS = 1e-6
D_IN = 2 * HG_QK + 2 * HG_WIDTH + 3 * SB_WIDTH

kernel_name = "hymba_hgrn2_stickbreaking_convffn_step"


def rmsnorm(x, g):
    xf = x.astype(jnp.float32)
    y = xf * lax.rsqrt(jnp.mean(xf * xf, axis=-1, keepdims=True) + RMS_EPS)
    return (y * g.astype(jnp.float32)).astype(x.dtype)


def hgrn2_recurrence(q, logf, k, v, s0, chunk):
    n, t, h, dk = q.shape
    dv = v.shape[-1]
    nc = t // chunk

    def to_chunks(a):
        return a.reshape(n, nc, chunk, h, a.shape[-1]).transpose(1, 0, 2, 3, 4)

    causal = jnp.tril(jnp.ones((chunk, chunk), dtype=bool))[None, :, :, None, None]

    def step(s, inp):
        qc, lfc, kc, vc = inp
        b = jnp.cumsum(lfc, axis=1)
        o_inter = jnp.einsum('nthk,nhkv->nthv', qc * jnp.exp(b), s)
        diff = jnp.where(causal, b[:, :, None] - b[:, None, :], 0.0)
        decay = jnp.where(causal, jnp.exp(diff), 0.0)
        att = jnp.einsum('nthk,ntshk,nshk->nhts', qc, decay, kc)
        o_intra = jnp.einsum('nhts,nshv->nthv', att, vc)
        b_last = b[:, -1]
        s_new = jnp.exp(b_last)[..., None] * s + jnp.einsum(
            'nshk,nshv->nhkv', kc * jnp.exp(b_last[:, None] - b), vc)
        return s_new, o_inter + o_intra

    s_t, o = lax.scan(step, s0, (to_chunks(q), to_chunks(logf), to_chunks(k), to_chunks(v)))
    return o.transpose(1, 0, 2, 3, 4).reshape(n, t, h, dv), s_t


def sb_attend(q, k, v, bias, q_pos, k_pos):
    z = jnp.einsum('nqhd,nkhd->nhqk', q.astype(jnp.float32), k.astype(jnp.float32)) * SB_SCALE
    z = z + bias.astype(jnp.float32)[None, :, None, None]
    strict = (k_pos[None, :] < q_pos[:, None])[None, None]
    c = jnp.where(strict, jax.nn.log_sigmoid(-z), 0.0)
    later = lax.cumsum(c, axis=3, reverse=True) - c
    a = jnp.where(strict, jnp.exp(jax.nn.log_sigmoid(z) + later), 0.0)
    return jnp.einsum('nhqk,nkhd->nqhd', a, v.astype(jnp.float32))


def stick_breaking(q, k, v, bias, q_offset):
    n, t, h, d = q.shape
    qb = min(SB_QBLOCK, t)
    nb = t // qb
    k_pos = jnp.arange(k.shape[1])

    def one_block(i):
        q_blk = lax.dynamic_slice_in_dim(q, i * qb, qb, axis=1)
        q_pos = q_offset + i * qb + jnp.arange(qb)
        return sb_attend(q_blk, k, v, bias, q_pos, k_pos)

    o = lax.map(one_block, jnp.arange(nb))
    return o.transpose(1, 0, 2, 3, 4).reshape(n, t, h, d)


def token_mix(h, p, lb, s0, past_k, past_v, chunk):
    n, t, _ = h.shape
    f32 = jnp.float32
    proj = h @ p['w_in']
    sizes = [HG_QK, HG_QK, HG_WIDTH, HG_WIDTH, SB_WIDTH, SB_WIDTH, SB_WIDTH]
    cuts = [int(c) for c in np.cumsum(sizes)[:-1]]
    hq, hf, hi, hgate, sq, sk, sv = jnp.split(proj, cuts, axis=-1)
    q = hq.reshape(n, t, HG_HEADS, HG_DK).astype(f32)
    f = lb + (1.0 - lb) * jax.nn.sigmoid(hf.reshape(n, t, HG_HEADS, HG_DK).astype(f32))
    vin = hi.reshape(n, t, HG_HEADS, HG_DV).astype(f32)
    o_hg, s_t = hgrn2_recurrence(q, jnp.log(f), 1.0 - f, vin, s0.astype(f32), chunk)
    o_hg = o_hg * lax.rsqrt(jnp.mean(o_hg * o_hg, axis=-1, keepdims=True) + RMS_EPS)
    o_hg = o_hg.reshape(n, t, HG_WIDTH) * p['hg_norm'].astype(f32) * jax.nn.silu(hgate.astype(f32))
    qs = sq.reshape(n, t, SB_HEADS, SB_DH)
    ks = sk.reshape(n, t, SB_HEADS, SB_DH)
    vs = sv.reshape(n, t, SB_HEADS, SB_DH)
    if past_k is None:
        keys, vals, offset = ks, vs, 0
    else:
        keys = jnp.concatenate([past_k.astype(ks.dtype), ks], axis=1)
        vals = jnp.concatenate([past_v.astype(vs.dtype), vs], axis=1)
        offset = past_k.shape[1]
    o_sb = stick_breaking(qs, keys, vals, p['sb_bias'], offset).reshape(n, t, SB_WIDTH)
    mixed = jnp.concatenate([o_hg, o_sb], axis=-1).astype(h.dtype) @ p['w_o']
    return mixed, s_t, ks, vs


def cross_attend(h, mk, mv, p):
    n, t, _ = h.shape
    q = (h @ p['w_cq']).reshape(n, t, CA_HEADS, CA_DH)
    s = jnp.einsum('nthd,nmhd->nhtm', q.astype(jnp.float32), mk.astype(jnp.float32)) * (CA_DH ** -0.5)
    pr = jax.nn.softmax(s, axis=-1)
    o = jnp.einsum('nhtm,nmhd->nthd', pr, mv.astype(jnp.float32))
    return o.reshape(n, t, CA_HEADS * CA_DH).astype(h.dtype) @ p['w_co']


def conv_ffn(h, p, buf):
    t = h.shape[1]
    u = h @ p['w_up']
    ext = jnp.concatenate([buf.astype(u.dtype), u], axis=1)
    c = p['conv_b'] + sum(p['conv_w'][j] * ext[:, j:j + t] for j in range(CONV_W))
    gate, val = jnp.split(c, 2, axis=-1)
    y = (jax.nn.gelu(gate, approximate=True) * val) @ p['w_down']
    return y, ext[:, t:]


def layer_forward(x, p, lb, s0, past_k, past_v, buf, mk, mv, chunk):
    mixed, s_t, k_new, v_new = token_mix(rmsnorm(x, p['g_mix_pre']), p, lb, s0, past_k, past_v, chunk)
    x = x + rmsnorm(mixed, p['g_mix_post'])
    x = x + rmsnorm(cross_attend(rmsnorm(x, p['g_ca_pre']), mk, mv, p), p['g_ca_post'])
    y, buf_new = conv_ffn(rmsnorm(x, p['g_ffn_pre']), p, buf)
    x = x + rmsnorm(y, p['g_ffn_post'])
    return x, s_t, k_new, v_new, buf_new


def memory_kv(mem, g, w_ck, w_cv):
    n = mem.shape[0]
    mn = rmsnorm(mem, g)
    mk = (mn @ w_ck).reshape(n, N_MEM, CA_HEADS, CA_DH)
    mv = (mn @ w_cv).reshape(n, N_MEM, CA_HEADS, CA_DH)
    return mk, mv


def setup_inputs(seed: int = 0) -> dict:
    key = jax.random.key(seed)
    ks = jax.random.split(key, 32)
    f32 = jnp.float32

    def nrm(k, shape, scale):
        return jax.random.normal(k, shape, f32) * scale

    def gain(k, shape):
        return 1.0 + 0.1 * jax.random.normal(k, shape, f32)

    n_pages = PAST_LEN // PAGE_SIZE
    n_pool = (DEC_BATCH * n_pages * 5) // 4
    perm = jax.random.permutation(ks[0], n_pool)
    page_table = perm[:DEC_BATCH * n_pages].reshape(DEC_BATCH, n_pages).astype(jnp.int32)
    return {
        'x_prompt': nrm(ks[1], (BATCH, SEQ, D_MODEL), 1.0),
        'x_sample': nrm(ks[2], (DEC_BATCH, DEC_SEQ, D_MODEL), 1.0),
        'cache_sb_k': nrm(ks[3], (DEPTH, n_pool, PAGE_SIZE, SB_HEADS, SB_DH), 1.0),
        'cache_sb_v': nrm(ks[4], (DEPTH, n_pool, PAGE_SIZE, SB_HEADS, SB_DH), 1.0),
        'state_hgrn': nrm(ks[5], (DEPTH, DEC_BATCH, HG_HEADS, HG_DK, HG_DV), 0.5),
        'state_ffn_conv': nrm(ks[6], (DEPTH, DEC_BATCH, CONV_W - 1, 2 * D_FF), 1.0),
        'cache_mem_k': nrm(ks[7], (DEPTH, DEC_BATCH, N_MEM, CA_HEADS, CA_DH), 1.0),
        'cache_mem_v': nrm(ks[8], (DEPTH, DEC_BATCH, N_MEM, CA_HEADS, CA_DH), 1.0),
        'page_table': page_table,
        'mem_prompt': nrm(ks[9], (BATCH, N_MEM, D_MODEL), 1.0),
        'w_in': nrm(ks[10], (DEPTH, D_MODEL, D_IN), D_MODEL ** -0.5),
        'hg_norm': gain(ks[11], (DEPTH, HG_WIDTH)),
        'hg_lb': nrm(ks[12], (DEPTH + 1, HG_QK), 0.5),
        'sb_bias': SB_BIAS_INIT + nrm(ks[29], (DEPTH, SB_HEADS), 0.5),
        'w_o': nrm(ks[13], (DEPTH, D_MIX, D_MODEL), D_MIX ** -0.5),
        'g_mix_pre': gain(ks[14], (DEPTH, D_MODEL)),
        'g_mix_post': gain(ks[15], (DEPTH, D_MODEL)),
        'g_ca_pre': gain(ks[16], (DEPTH, D_MODEL)),
        'g_ca_post': gain(ks[17], (DEPTH, D_MODEL)),
        'g_mem': gain(ks[18], (DEPTH, D_MODEL)),
        'w_cq': nrm(ks[19], (DEPTH, D_MODEL, D_MODEL), D_MODEL ** -0.5),
        'w_ck': nrm(ks[20], (DEPTH, D_MODEL, D_MODEL), D_MODEL ** -0.5),
        'w_cv': nrm(ks[21], (DEPTH, D_MODEL, D_MODEL), D_MODEL ** -0.5),
        'w_co': nrm(ks[22], (DEPTH, D_MODEL, D_MODEL), D_MODEL ** -0.5),
        'g_ffn_pre': gain(ks[23], (DEPTH, D_MODEL)),
        'g_ffn_post': gain(ks[24], (DEPTH, D_MODEL)),
        'w_up': nrm(ks[25], (DEPTH, D_MODEL, 2 * D_FF), D_MODEL ** -0.5),
        'conv_w': nrm(ks[26], (DEPTH, CONV_W, 2 * D_FF), CONV_W ** -0.5),
        'conv_b': nrm(ks[27], (DEPTH, 2 * D_FF), 0.01),
        'w_down': nrm(ks[28], (DEPTH, D_FF, D_MODEL), D_FF ** -0.5),
    }


def reference(x_prompt, x_sample, cache_sb_k, cache_sb_v, state_hgrn, state_ffn_conv,
              cache_mem_k, cache_mem_v, page_table, mem_prompt,
              w_in, hg_norm, hg_lb, sb_bias, w_o, g_mix_pre, g_mix_post, g_ca_pre, g_ca_post, g_mem,
              w_cq, w_ck, w_cv, w_co, g_ffn_pre, g_ffn_post, w_up, conv_w, conv_b, w_down):
    n_prompt, seq_len, _ = x_prompt.shape
    n_dec, dec_len, _ = x_sample.shape
    n_pages = page_table.shape[1]
    past_len = n_pages * PAGE_SIZE
    lower_bounds = jnp.cumsum(jax.nn.softmax(hg_lb.astype(jnp.float32), axis=0), axis=0)

    yp, ys = x_prompt, x_sample
    kp_l, vp_l, sp_l, bp_l, mkp_l, mvp_l = [], [], [], [], [], []
    ks_l, vs_l, ss_l, bs_l = [], [], [], []
    for l in range(DEPTH):
        p = {'w_in': w_in[l], 'hg_norm': hg_norm[l], 'sb_bias': sb_bias[l], 'w_o': w_o[l],
             'g_mix_pre': g_mix_pre[l], 'g_mix_post': g_mix_post[l],
             'g_ca_pre': g_ca_pre[l], 'g_ca_post': g_ca_post[l],
             'w_cq': w_cq[l], 'w_co': w_co[l],
             'g_ffn_pre': g_ffn_pre[l], 'g_ffn_post': g_ffn_post[l],
             'w_up': w_up[l], 'conv_w': conv_w[l], 'conv_b': conv_b[l], 'w_down': w_down[l]}
        lb = lower_bounds[l].reshape(HG_HEADS, HG_DK)

        mk_p, mv_p = memory_kv(mem_prompt, g_mem[l], w_ck[l], w_cv[l])
        s0_p = jnp.zeros((n_prompt, HG_HEADS, HG_DK, HG_DV), jnp.float32)
        buf_p = jnp.zeros((n_prompt, CONV_W - 1, 2 * D_FF), x_prompt.dtype)
        yp, s_p, k_p, v_p, nb_p = layer_forward(yp, p, lb, s0_p, None, None, buf_p,
                                                mk_p, mv_p, min(HG_CHUNK, seq_len))

        past_k = cache_sb_k[l][page_table].reshape(n_dec, past_len, SB_HEADS, SB_DH)
        past_v = cache_sb_v[l][page_table].reshape(n_dec, past_len, SB_HEADS, SB_DH)
        ys, s_s, k_s, v_s, nb_s = layer_forward(ys, p, lb, state_hgrn[l], past_k, past_v,
                                                state_ffn_conv[l], cache_mem_k[l], cache_mem_v[l],
                                                dec_len)

        kp_l.append(k_p); vp_l.append(v_p); sp_l.append(s_p); bp_l.append(nb_p)
        mkp_l.append(mk_p); mvp_l.append(mv_p)
        ks_l.append(k_s); vs_l.append(v_s); ss_l.append(s_s); bs_l.append(nb_s)

    return (yp, ys,
            jnp.stack(kp_l), jnp.stack(vp_l), jnp.stack(sp_l), jnp.stack(bp_l),
            jnp.stack(mkp_l), jnp.stack(mvp_l),
            jnp.stack(ks_l), jnp.stack(vs_l), jnp.stack(ss_l), jnp.stack(bs_l))
```

```python
import functools

import numpy as np
import jax
import jax.numpy as jnp
from jax import lax
from jax.experimental import pallas as pl
from jax.experimental.pallas import tpu as pltpu

f32 = jnp.float32
bf16 = jnp.bfloat16

D_MODEL = 1024
HG_HEADS = 8
HG_DK = 64
HG_W = HG_HEADS * HG_DK
SB_HEADS = 8
SB_DH = 64
SB_W = SB_HEADS * SB_DH
SB_SCALE = SB_DH ** -0.5
N_MEM = 256
CA_HEADS = 4
CA_DH = D_MODEL // CA_HEADS
CA_SCALE = CA_DH ** -0.5
D_FF = 2816
RMS_EPS = 1e-6
PAGE = 128
LANES = 128
VMEM_LIMIT = 56 * 1024 * 1024


def _cparams(sem):
    return pltpu.CompilerParams(dimension_semantics=sem, vmem_limit_bytes=VMEM_LIMIT)


def _const_spec(shape):
    nd = len(shape)
    return pl.BlockSpec(shape, lambda *_: (0,) * nd, pipeline_mode=pl.Buffered(1))


def _rms(x, g):
    return x * lax.rsqrt(jnp.mean(x * x, axis=-1, keepdims=True) + RMS_EPS) * g


def _dot(a, b):
    return jnp.dot(a, b, preferred_element_type=f32)


def _dot_nt(a, b):
    return lax.dot_general(a, b, (((1,), (1,)), ((), ())), preferred_element_type=f32)


def _dot_tn(a, b):
    return lax.dot_general(a, b, (((0,), (0,)), ((), ())), preferred_element_type=f32)


def _in_proj_body(x_ref, g_ref, w_ref, hin_ref, sk_ref, sv_ref, qn_ref, kb_ref, vb_ref):
    xn = _rms(x_ref[...], g_ref[...]).astype(bf16)
    for j in range(4):
        hin_ref[:, j * HG_W:(j + 1) * HG_W] = _dot(xn, w_ref[:, j * HG_W:(j + 1) * HG_W])
    base = 4 * HG_W
    q = _dot(xn, w_ref[:, base:base + SB_W])
    qn_ref[...] = (q * (-SB_SCALE)).astype(bf16)
    k = _dot(xn, w_ref[:, base + SB_W:base + 2 * SB_W])
    sk_ref[...] = k
    kb_ref[...] = k.astype(bf16)
    v = _dot(xn, w_ref[:, base + 2 * SB_W:base + 3 * SB_W])
    sv_ref[...] = v
    vb_ref[...] = v.astype(bf16)


def _in_proj(x, g, w_bf, tm):
    t = x.shape[0]
    d_in = w_bf.shape[1]
    row = lambda n: pl.BlockSpec((tm, n), lambda i: (i, 0))
    return pl.pallas_call(
        _in_proj_body,
        grid=(t // tm,),
        in_specs=[row(D_MODEL), _const_spec((1, D_MODEL)), _const_spec((D_MODEL, d_in))],
        out_specs=[row(4 * HG_W), row(SB_W), row(SB_W), row(SB_W), row(SB_W), row(SB_W)],
        out_shape=[jax.ShapeDtypeStruct((t, 4 * HG_W), f32),
                   jax.ShapeDtypeStruct((t, SB_W), f32), jax.ShapeDtypeStruct((t, SB_W), f32),
                   jax.ShapeDtypeStruct((t, SB_W), bf16), jax.ShapeDtypeStruct((t, SB_W), bf16),
                   jax.ShapeDtypeStruct((t, SB_W), bf16)],
        compiler_params=_cparams(("arbitrary",)),
        name="in_proj",
    )(x, g, w_bf)


def _hgrn_consts(nb, c):
    r = np.arange(nb)
    same = (r[:, None] // c) == (r[None, :] // c)
    tri = same & (r[None, :] <= r[:, None])
    stack = np.concatenate([tri, same], axis=0).astype(np.float32)
    h = np.arange(HG_W) // HG_DK
    bo = (h[:, None] == h[None, :]).astype(np.float32)
    return jnp.asarray(stack, bf16), jnp.asarray(bo, bf16)


def _hgrn_prep(hin_ref, lbp_ref, stack_ref, scr, nb):
    q_scr, k_scr, v_scr, b_scr, qe_scr, kt_scr, el_scr = scr
    a = lbp_ref[...]
    e = jnp.exp(a - jnp.max(a, axis=0, keepdims=True))
    lb = e[0:1] / jnp.sum(e, axis=0, keepdims=True)
    hq = hin_ref[:, 0:HG_W]
    f = lb + (1.0 - lb) * jax.nn.sigmoid(hin_ref[:, HG_W:2 * HG_W])
    g = jnp.log(f)
    kk = 1.0 - f
    g1 = g.astype(bf16)
    r1 = g - g1.astype(f32)
    g2 = r1.astype(bf16)
    g3 = (r1 - g2.astype(f32)).astype(bf16)
    st = stack_ref[...]
    bb = _dot(st, g1) + _dot(st, g2) + _dot(st, g3)
    b = bb[:nb]
    btot = bb[nb:]
    q_scr[...] = hq
    k_scr[...] = kk
    v_scr[...] = hin_ref[:, 2 * HG_W:3 * HG_W]
    b_scr[...] = b
    qe_scr[...] = hq * jnp.exp(b)
    kt_scr[...] = kk * jnp.exp(btot - b)
    el_scr[...] = jnp.exp(btot)


def _hgrn_chunk(scr, bo, r0, c, st_list):
    q_scr, k_scr, v_scr, b_scr, qe_scr, kt_scr, el_scr = scr
    rows = pl.ds(r0, c)
    b_c = b_scr[rows, :]
    q_c = q_scr[rows, :]
    k_c = k_scr[rows, :]
    v_c = v_scr[rows, :]
    qe_c = qe_scr[rows, :].astype(bf16)
    kt_c = kt_scr[rows, :].astype(bf16)
    el_c = el_scr[pl.ds(r0, 1), :]
    v_cb = v_c.astype(bf16)
    t_idx = lax.broadcasted_iota(jnp.int32, (c, 1), 0)
    slabs = []
    for s in range(c):
        d = jnp.exp(jnp.minimum(b_c - b_c[s:s + 1, :], 0.0))
        slabs.append(jnp.where(t_idx >= s, q_c * d * k_c[s:s + 1, :], 0.0))
    p = jnp.concatenate(slabs, axis=0).astype(bf16)
    pw = _dot(p, bo)
    o = pw[0:c, :] * v_c[0:1, :]
    for s in range(1, c):
        o = o + pw[s * c:(s + 1) * c, :] * v_c[s:s + 1, :]
    ri = lax.broadcasted_iota(jnp.int32, (LANES, LANES), 0) // HG_DK
    ci = lax.broadcasted_iota(jnp.int32, (LANES, LANES), 1) // HG_DK
    same_head = ri == ci
    o_parts, new_states = [], []
    for pr in range(HG_W // LANES):
        sl = slice(pr * LANES, (pr + 1) * LANES)
        st = st_list[pr]
        o_parts.append(_dot_nt(qe_c[:, sl], st.astype(bf16)))
        upd = _dot_tn(v_cb[:, sl], kt_c[:, sl])
        new_states.append(st * el_c[:, sl] + jnp.where(same_head, upd, 0.0))
    return o + jnp.concatenate(o_parts, axis=1), new_states


def _hgrn_finish(o, hin_ref, gn_ref, bo, o_ref):
    ms = _dot((o * o).astype(bf16), bo) * (1.0 / HG_DK)
    gate = hin_ref[:, 3 * HG_W:4 * HG_W]
    o_ref[...] = (o * lax.rsqrt(ms + RMS_EPS) * gn_ref[...] * (gate * jax.nn.sigmoid(gate))).astype(o_ref.dtype)


def _pair_state_out(st):
    t = st.T
    return t[0:HG_DK, 0:HG_DK], t[HG_DK:LANES, HG_DK:LANES]


def _hgrn_prompt_body(hin_ref, lbp_ref, gn_ref, stack_ref, bo_ref, o_ref, sout_ref,
                      st_scr, oacc_scr, *scr, nb, c):
    j = pl.program_id(1)

    @pl.when(j == 0)
    def _():
        st_scr[...] = jnp.zeros_like(st_scr)

    _hgrn_prep(hin_ref, lbp_ref, stack_ref, scr, nb)
    bo = bo_ref[...]

    def step(ci, carry):
        r0 = pl.multiple_of(ci * c, c)
        o, new_states = _hgrn_chunk(scr, bo, r0, c, [st_scr[pr] for pr in range(HG_W // LANES)])
        for pr, st in enumerate(new_states):
            st_scr[pr] = st
        oacc_scr[pl.ds(r0, c), :] = o
        return carry

    lax.fori_loop(0, nb // c, step, 0)
    _hgrn_finish(oacc_scr[...], hin_ref, gn_ref, bo, o_ref)

    @pl.when(j == pl.num_programs(1) - 1)
    def _():
        for pr in range(HG_W // LANES):
            sa, sb = _pair_state_out(st_scr[pr])
            sout_ref[0, 2 * pr] = sa
            sout_ref[0, 2 * pr + 1] = sb


def _hgrn_prompt(hin, lbp, gn, n_seq, seq_len, nb, c):
    stack, bo = _hgrn_consts(nb, c)
    nblk = seq_len // nb
    scr = [pltpu.VMEM((nb, HG_W), f32) for _ in range(7)]
    return pl.pallas_call(
        functools.partial(_hgrn_prompt_body, nb=nb, c=c),
        grid=(n_seq, nblk),
        in_specs=[pl.BlockSpec((nb, 4 * HG_W), lambda b, j: (b * nblk + j, 0)),
                  _const_spec(lbp.shape), _const_spec((1, HG_W)),
                  _const_spec(stack.shape), _const_spec(bo.shape)],
        out_specs=[pl.BlockSpec((nb, HG_W), lambda b, j: (b * nblk + j, 0)),
                   pl.BlockSpec((1, HG_HEADS, HG_DK, HG_DK), lambda b, j: (b, 0, 0, 0))],
        out_shape=[jax.ShapeDtypeStruct((n_seq * seq_len, HG_W), bf16),
                   jax.ShapeDtypeStruct((n_seq, HG_HEADS, HG_DK, HG_DK), f32)],
        scratch_shapes=[pltpu.VMEM((HG_W // LANES, LANES, LANES), f32), pltpu.VMEM((nb, HG_W), f32)] + scr,
        compiler_params=_cparams(("arbitrary", "arbitrary")),
        name="hgrn_prompt",
    )(hin, lbp, gn, stack, bo)


def _hgrn_sample_body(hin_ref, lbp_ref, gn_ref, stack_ref, bo_ref, s0_ref, o_ref, sout_ref,
                      oacc_scr, *scr, nb, c):
    _hgrn_prep(hin_ref, lbp_ref, stack_ref, scr, nb)
    bo = bo_ref[...]
    zero = jnp.zeros((HG_DK, HG_DK), f32)

    def step(ci, carry):
        r0 = pl.multiple_of(ci * c, c)
        states = []
        for pr in range(HG_W // LANES):
            sa = s0_ref[ci, 2 * pr]
            sb = s0_ref[ci, 2 * pr + 1]
            bd = jnp.concatenate([jnp.concatenate([sa, zero], axis=1),
                                  jnp.concatenate([zero, sb], axis=1)], axis=0)
            states.append(bd.T)
        o, new_states = _hgrn_chunk(scr, bo, r0, c, states)
        for pr, st in enumerate(new_states):
            sa, sb = _pair_state_out(st)
            sout_ref[ci, 2 * pr] = sa
            sout_ref[ci, 2 * pr + 1] = sb
        oacc_scr[pl.ds(r0, c), :] = o
        return carry

    lax.fori_loop(0, nb // c, step, 0)
    _hgrn_finish(oacc_scr[...], hin_ref, gn_ref, bo, o_ref)


def _hgrn_sample(hin, lbp, gn, s0, c, seqs_per_step):
    n_seq = s0.shape[0]
    nb = seqs_per_step * c
    stack, bo = _hgrn_consts(nb, c)
    scr = [pltpu.VMEM((nb, HG_W), f32) for _ in range(7)]
    st_spec = pl.BlockSpec((seqs_per_step, HG_HEADS, HG_DK, HG_DK), lambda i: (i, 0, 0, 0))
    return pl.pallas_call(
        functools.partial(_hgrn_sample_body, nb=nb, c=c),
        grid=(n_seq // seqs_per_step,),
        in_specs=[pl.BlockSpec((nb, 4 * HG_W), lambda i: (i, 0)),
                  _const_spec(lbp.shape), _const_spec((1, HG_W)),
                  _const_spec(stack.shape), _const_spec(bo.shape), st_spec],
        out_specs=[pl.BlockSpec((nb, HG_W), lambda i: (i, 0)), st_spec],
        out_shape=[jax.ShapeDtypeStruct((n_seq * c, HG_W), bf16),
                   jax.ShapeDtypeStruct(s0.shape, f32)],
        scratch_shapes=[pltpu.VMEM((nb, HG_W), f32)] + scr,
        compiler_params=_cparams(("arbitrary",)),
        name="hgrn_sample",
    )(hin, lbp, gn, stack, bo, s0)


def _sb_block(qh, kblk, vblk, nbias, u, r, acc, strict):
    zp = _dot_nt(qh, kblk) + nbias
    l = jnp.log(1.0 + jnp.exp(-jnp.abs(zp)))
    c = jnp.minimum(zp, 0.0) - l
    lsz = c - zp
    if strict is not None:
        c = jnp.where(strict, c, 0.0)
    local = _dot(c.astype(bf16), u)
    a = jnp.exp(lsz + local)
    if strict is not None:
        a = jnp.where(strict, a, 0.0)
    acc = acc + jnp.exp(r) * _dot(a.astype(bf16), vblk)
    r = r + jnp.sum(c, axis=-1, keepdims=True)
    return r, acc


def _sb_prompt_body(bias_ref, q_ref, k_ref, v_ref, u_ref, o_ref, *, tq):
    pr = pl.program_id(1)
    i = pl.program_id(2)
    lane = lax.broadcasted_iota(jnp.int32, (1, LANES), 1)
    q = q_ref[...]
    zero_q = jnp.zeros_like(q)
    qa = jnp.where(lane < SB_DH, q, zero_q)
    qb = jnp.where(lane >= SB_DH, q, zero_q)
    nba = -bias_ref[2 * pr]
    nbb = -bias_ref[2 * pr + 1]
    u = u_ref[...]
    rr = lax.broadcasted_iota(jnp.int32, (tq, tq), 0)
    cc = lax.broadcasted_iota(jnp.int32, (tq, tq), 1)
    strict = cc < rr
    r0 = jnp.zeros((tq, 1), f32)
    acc0 = jnp.zeros((tq, LANES), f32)
    d0 = pl.multiple_of(i * tq, tq)
    kd = k_ref[pl.ds(d0, tq), :]
    vd = v_ref[pl.ds(d0, tq), :]
    ra, acca = _sb_block(qa, kd, vd, nba, u, r0, acc0, strict)
    rb, accb = _sb_block(qb, kd, vd, nbb, u, r0, acc0, strict)

    def step(it, carry):
        ra, acca, rb, accb = carry
        k0 = pl.multiple_of((i - 1 - it) * tq, tq)
        kb = k_ref[pl.ds(k0, tq), :]
        vb = v_ref[pl.ds(k0, tq), :]
        ra, acca = _sb_block(qa, kb, vb, nba, u, ra, acca, None)
        rb, accb = _sb_block(qb, kb, vb, nbb, u, rb, accb, None)
        return ra, acca, rb, accb

    ra, acca, rb, accb = lax.fori_loop(0, i, step, (ra, acca, rb, accb))
    o_ref[...] = jnp.where(lane < SB_DH, acca, accb).astype(o_ref.dtype)


def _strict_upper(n):
    r = np.arange(n)
    return jnp.asarray((r[:, None] > r[None, :]).astype(np.float32), bf16)


def _sb_prompt(qn, kb, vb, bias, n_seq, seq_len, tq):
    nq = seq_len // tq
    u = _strict_upper(tq)
    return pl.pallas_call(
        functools.partial(_sb_prompt_body, tq=tq),
        grid=(n_seq, SB_W // LANES, nq),
        in_specs=[pl.BlockSpec(memory_space=pltpu.SMEM),
                  pl.BlockSpec((tq, LANES), lambda b, p, i: (b * nq + i, p)),
                  pl.BlockSpec((seq_len, LANES), lambda b, p, i: (b, p)),
                  pl.BlockSpec((seq_len, LANES), lambda b, p, i: (b, p)),
                  _const_spec(u.shape)],
        out_specs=pl.BlockSpec((tq, LANES), lambda b, p, i: (b * nq + i, p)),
        out_shape=jax.ShapeDtypeStruct((n_seq * seq_len, SB_W), bf16),
        compiler_params=_cparams(("arbitrary", "arbitrary", "arbitrary")),
        name="sb_prompt",
    )(bias, qn, kb, vb, u)


def _sb_sample_body(pt_ref, bias_ref, q_ref, kn_ref, vn_ref, u_ref, *rest, n_pages, dec):
    k_refs = rest[:n_pages]
    v_refs = rest[n_pages:2 * n_pages]
    o_ref = rest[2 * n_pages]
    rows = SB_HEADS * dec
    lane_head = lax.broadcasted_iota(jnp.int32, (1, SB_W), 1) // SB_DH
    q = q_ref[0].astype(f32)
    qbd = jnp.concatenate([jnp.where(lane_head == h, q, 0.0) for h in range(SB_HEADS)], axis=0).astype(bf16)
    row_head = lax.broadcasted_iota(jnp.int32, (rows, 1), 0) // dec
    nbias = jnp.zeros((rows, 1), f32)
    for h in range(SB_HEADS):
        nbias = jnp.where(row_head == h, -bias_ref[h], nbias)
    u = u_ref[...]
    pad = jnp.zeros((PAGE - dec, SB_W), f32)
    kn = jnp.concatenate([kn_ref[0].astype(f32), pad], axis=0).astype(bf16)
    vn = jnp.concatenate([vn_ref[0].astype(f32), pad], axis=0).astype(bf16)
    t_row = lax.broadcasted_iota(jnp.int32, (rows, PAGE), 0) % dec
    s_col = lax.broadcasted_iota(jnp.int32, (rows, PAGE), 1)
    r = jnp.zeros((rows, 1), f32)
    acc = jnp.zeros((rows, SB_W), f32)
    r, acc = _sb_block(qbd, kn, vn, nbias, u, r, acc, s_col < t_row)
    for j in range(n_pages - 1, -1, -1):
        r, acc = _sb_block(qbd, k_refs[j][0].astype(bf16), v_refs[j][0].astype(bf16), nbias, u, r, acc, None)
    out = jnp.zeros((dec, SB_W), f32)
    for h in range(SB_HEADS):
        out = out + jnp.where(lane_head == h, acc[h * dec:(h + 1) * dec, :], 0.0)
    o_ref[0] = out.astype(o_ref.dtype)


def _sb_sample(qn, kb, vb, cache_k, cache_v, page_table, bias, dec):
    n_seq, n_pages = page_table.shape
    u = _strict_upper(PAGE)
    pt = page_table.reshape(-1)
    tok = pl.BlockSpec((1, dec, SB_W), lambda n, pt: (n, 0, 0))

    def page_spec(j):
        return pl.BlockSpec((1, PAGE, SB_W), lambda n, pt: (pt[n * n_pages + j], 0, 0))

    grid_spec = pltpu.PrefetchScalarGridSpec(
        num_scalar_prefetch=1,
        grid=(n_seq,),
        in_specs=[pl.BlockSpec(memory_space=pltpu.SMEM), tok, tok, tok,
                  pl.BlockSpec(u.shape, lambda n, pt: (0, 0))]
                 + [page_spec(j) for j in range(n_pages)] * 2,
        out_specs=tok,
    )
    return pl.pallas_call(
        functools.partial(_sb_sample_body, n_pages=n_pages, dec=dec),
        grid_spec=grid_spec,
        out_shape=jax.ShapeDtypeStruct((n_seq, dec, SB_W), bf16),
        compiler_params=_cparams(("arbitrary",)),
        name="sb_sample",
    )(pt, bias, qn.reshape(n_seq, dec, SB_W), kb.reshape(n_seq, dec, SB_W), vb.reshape(n_seq, dec, SB_W), u,
      *([cache_k] * n_pages), *([cache_v] * n_pages))


def _mix_out_body(x_ref, ohg_ref, osb_ref, wo_ref, g1_ref, g2_ref, wq_ref, x1_ref, qc_ref):
    mixed = _dot(ohg_ref[...], wo_ref[0:HG_W, :]) + _dot(osb_ref[...], wo_ref[HG_W:HG_W + SB_W, :])
    x1 = x_ref[...] + _rms(mixed, g1_ref[...])
    x1_ref[...] = x1
    h2 = _rms(x1, g2_ref[...]).astype(bf16)
    qc_ref[...] = (_dot(h2, wq_ref[...]) * CA_SCALE).astype(bf16)


def _mix_out(x, ohg, osb, wo_bf, g1, g2, wq_bf, tm):
    t = x.shape[0]
    row = lambda n: pl.BlockSpec((tm, n), lambda i: (i, 0))
    return pl.pallas_call(
        _mix_out_body,
        grid=(t // tm,),
        in_specs=[row(D_MODEL), row(HG_W), row(SB_W), _const_spec(wo_bf.shape),
                  _const_spec((1, D_MODEL)), _const_spec((1, D_MODEL)), _const_spec(wq_bf.shape)],
        out_specs=[row(D_MODEL), row(D_MODEL)],
        out_shape=[jax.ShapeDtypeStruct((t, D_MODEL), f32), jax.ShapeDtypeStruct((t, D_MODEL), bf16)],
        compiler_params=_cparams(("arbitrary",)),
        name="mix_out",
    )(x, ohg, osb, wo_bf, g1, g2, wq_bf)


def _mem_kv_body(m_ref, g_ref, wk_ref, wv_ref, mk_ref, mv_ref):
    mn = _rms(m_ref[...], g_ref[...]).astype(bf16)
    mk_ref[...] = _dot(mn, wk_ref[...])
    mv_ref[...] = _dot(mn, wv_ref[...])


def _mem_kv(mem, g, wk_bf, wv_bf, tm):
    t = mem.shape[0]
    row = pl.BlockSpec((tm, D_MODEL), lambda i: (i, 0))
    return pl.pallas_call(
        _mem_kv_body,
        grid=(t // tm,),
        in_specs=[row, _const_spec((1, D_MODEL)), _const_spec(wk_bf.shape), _const_spec(wv_bf.shape)],
        out_specs=[row, row],
        out_shape=[jax.ShapeDtypeStruct((t, D_MODEL), f32)] * 2,
        compiler_params=_cparams(("arbitrary",)),
        name="mem_kv",
    )(mem, g, wk_bf, wv_bf)


def _cross_heads(q, mk, mv):
    outs = []
    for h in range(CA_HEADS):
        sl = slice(h * CA_DH, (h + 1) * CA_DH)
        s = _dot_nt(q[:, sl], mk[:, sl])
        s = s - jnp.max(s, axis=-1, keepdims=True)
        e = jnp.exp(s)
        p = e / jnp.sum(e, axis=-1, keepdims=True)
        outs.append(_dot(p.astype(bf16), mv[:, sl]))
    return jnp.concatenate(outs, axis=1)


def _cross_prompt_body(q_ref, mk_ref, mv_ref, o_ref):
    o_ref[...] = _cross_heads(q_ref[...], mk_ref[...].astype(bf16), mv_ref[...].astype(bf16)).astype(o_ref.dtype)


def _cross_prompt(qc, mk, mv, n_seq, seq_len, tm):
    nblk = seq_len // tm
    row = pl.BlockSpec((tm, D_MODEL), lambda b, j: (b * nblk + j, 0))
    mem = pl.BlockSpec((N_MEM, D_MODEL), lambda b, j: (b, 0))
    return pl.pallas_call(
        _cross_prompt_body,
        grid=(n_seq, nblk),
        in_specs=[row, mem, mem],
        out_specs=row,
        out_shape=jax.ShapeDtypeStruct(qc.shape, bf16),
        compiler_params=_cparams(("arbitrary", "arbitrary")),
        name="cross_prompt",
    )(qc, mk, mv)


def _cross_sample_body(q_ref, mk_ref, mv_ref, o_ref, *, group, dec):
    for s in range(group):
        o_ref[s * dec:(s + 1) * dec, :] = _cross_heads(
            q_ref[s * dec:(s + 1) * dec, :], mk_ref[s].astype(bf16), mv_ref[s].astype(bf16)).astype(o_ref.dtype)


def _cross_sample(qc, mk, mv, dec, group):
    n_seq = mk.shape[0]
    row = pl.BlockSpec((group * dec, D_MODEL), lambda i: (i, 0))
    mem = pl.BlockSpec((group, N_MEM, D_MODEL), lambda i: (i, 0, 0))
    return pl.pallas_call(
        functools.partial(_cross_sample_body, group=group, dec=dec),
        grid=(n_seq // group,),
        in_specs=[row, mem, mem],
        out_specs=row,
        out_shape=jax.ShapeDtypeStruct(qc.shape, bf16),
        compiler_params=_cparams(("arbitrary",)),
        name="cross_sample",
    )(qc, mk, mv)


def _ca_out_body(ca_ref, x1_ref, wco_ref, g1_ref, g2_ref, x2_ref, h3_ref):
    x2 = x1_ref[...] + _rms(_dot(ca_ref[...], wco_ref[...]), g1_ref[...])
    x2_ref[...] = x2
    h3_ref[...] = _rms(x2, g2_ref[...]).astype(bf16)


def _ca_out(ca, x1, wco_bf, g1, g2, tm):
    t = x1.shape[0]
    row = pl.BlockSpec((tm, D_MODEL), lambda i: (i, 0))
    return pl.pallas_call(
        _ca_out_body,
        grid=(t // tm,),
        in_specs=[row, row, _const_spec(wco_bf.shape), _const_spec((1, D_MODEL)), _const_spec((1, D_MODEL))],
        out_specs=[row, row],
        out_shape=[jax.ShapeDtypeStruct((t, D_MODEL), f32), jax.ShapeDtypeStruct((t, D_MODEL), bf16)],
        compiler_params=_cparams(("arbitrary",)),
        name="ca_out",
    )(ca, x1, wco_bf, g1, g2)


FFN_CHUNK = 256


def _ffn_body(*refs, tm, dec, carry_tail):
    if carry_tail:
        h3_ref, x2_ref, wup_ref, cw_ref, cb_ref, wdn_ref, g_ref, y_ref, tail_ref, tail_scr, act_scr = refs
        halo_ref = None

        @pl.when(pl.program_id(1) == 0)
        def _():
            tail_scr[...] = jnp.zeros_like(tail_scr)
    else:
        h3_ref, x2_ref, halo_ref, wup_ref, cw_ref, cb_ref, wdn_ref, g_ref, y_ref, u_ref, act_scr = refs
        pos = lax.broadcasted_iota(jnp.int32, (tm, 1), 0) % dec
    h3 = h3_ref[...]

    def conv_chunk(col):
        cs = slice(col, col + FFN_CHUNK)
        u = _dot(h3, wup_ref[:, cs])
        if carry_tail:
            ext = jnp.concatenate([tail_scr[:, cs], u], axis=0)
            u1 = ext[7:7 + tm]
            u2 = ext[6:6 + tm]
            tail_scr[:, cs] = u[tm - 8:tm]
            tail_ref[:, cs] = u[tm - 8:tm]
        else:
            ext = jnp.concatenate([jnp.zeros((8, FFN_CHUNK), f32), u], axis=0)
            hal = halo_ref[:, cs]
            hext = jnp.concatenate([hal, jnp.zeros((8, FFN_CHUNK), f32)], axis=0)
            u1 = jnp.where(pos == 0, hext[1:1 + tm], ext[7:7 + tm])
            u2 = jnp.where(pos < 2, hal, ext[6:6 + tm])
            u_ref[:, cs] = u
        return cb_ref[:, cs] + cw_ref[2:3, cs] * u + cw_ref[1:2, cs] * u1 + cw_ref[0:1, cs] * u2

    for cj in range(D_FF // FFN_CHUNK):
        gate = conv_chunk(cj * FFN_CHUNK)
        val = conv_chunk(D_FF + cj * FFN_CHUNK)
        act_scr[:, cj * FFN_CHUNK:(cj + 1) * FFN_CHUNK] = (jax.nn.gelu(gate, approximate=True) * val).astype(bf16)
    y = _dot(act_scr[...], wdn_ref[...])
    y_ref[...] = x2_ref[...] + _rms(y, g_ref[...])


def _ffn_prompt(h3, x2, wup_bf, cw, cb, wdn_bf, g, n_seq, seq_len, tm):
    nblk = seq_len // tm
    row = pl.BlockSpec((tm, D_MODEL), lambda b, j: (b * nblk + j, 0))
    return pl.pallas_call(
        functools.partial(_ffn_body, tm=tm, dec=None, carry_tail=True),
        grid=(n_seq, nblk),
        in_specs=[row, row, _const_spec(wup_bf.shape), _const_spec(cw.shape), _const_spec(cb.shape),
                  _const_spec(wdn_bf.shape), _const_spec((1, D_MODEL))],
        out_specs=[row, pl.BlockSpec((8, 2 * D_FF), lambda b, j: (b, 0))],
        out_shape=[jax.ShapeDtypeStruct((n_seq * seq_len, D_MODEL), f32),
                   jax.ShapeDtypeStruct((n_seq * 8, 2 * D_FF), f32)],
        scratch_shapes=[pltpu.VMEM((8, 2 * D_FF), f32), pltpu.VMEM((tm, D_FF), bf16)],
        compiler_params=_cparams(("arbitrary", "arbitrary")),
        name="ffn_prompt",
    )(h3, x2, wup_bf, cw, cb, wdn_bf, g)


def _ffn_sample(h3, x2, halo, wup_bf, cw, cb, wdn_bf, g, dec, tm):
    t = h3.shape[0]
    row = pl.BlockSpec((tm, D_MODEL), lambda i: (i, 0))
    wide = pl.BlockSpec((tm, 2 * D_FF), lambda i: (i, 0))
    return pl.pallas_call(
        functools.partial(_ffn_body, tm=tm, dec=dec, carry_tail=False),
        grid=(t // tm,),
        in_specs=[row, row, wide, _const_spec(wup_bf.shape), _const_spec(cw.shape), _const_spec(cb.shape),
                  _const_spec(wdn_bf.shape), _const_spec((1, D_MODEL))],
        out_specs=[row, wide],
        out_shape=[jax.ShapeDtypeStruct((t, D_MODEL), f32), jax.ShapeDtypeStruct((t, 2 * D_FF), f32)],
        scratch_shapes=[pltpu.VMEM((tm, D_FF), bf16)],
        compiler_params=_cparams(("arbitrary",)),
        name="ffn_sample",
    )(h3, x2, halo, wup_bf, cw, cb, wdn_bf, g)


def kernel(x_prompt, x_sample, cache_sb_k, cache_sb_v, state_hgrn, state_ffn_conv, cache_mem_k, cache_mem_v,
           page_table, mem_prompt, w_in, hg_norm, hg_lb, sb_bias, w_o, g_mix_pre, g_mix_post, g_ca_pre, g_ca_post,
           g_mem, w_cq, w_ck, w_cv, w_co, g_ffn_pre, g_ffn_post, w_up, conv_w, conv_b, w_down):
    n_p, seq_len, _ = x_prompt.shape
    n_d, dec, _ = x_sample.shape
    depth = w_in.shape[0]
    assert depth == 1, "single-layer step"
    assert dec >= 2 and dec % 8 == 0, "the conv tail is taken from the new rows"
    l = 0
    row = lambda a: a[l].reshape(1, -1)
    w_in_bf, w_o_bf, w_cq_bf, w_co_bf = (w[l].astype(bf16) for w in (w_in, w_o, w_cq, w_co))
    w_ck_bf, w_cv_bf, w_up_bf, w_dn_bf = (w[l].astype(bf16) for w in (w_ck, w_cv, w_up, w_down))
    lbp = hg_lb[l:l + 2]
    gn = row(hg_norm)
    bias = sb_bias[l]
    cb = row(conv_b)
    cw = conv_w[l]

    xp = x_prompt.reshape(n_p * seq_len, D_MODEL)
    xs = x_sample.reshape(n_d * dec, D_MODEL)

    hin, sk, sv, qn, kb, vb = _in_proj(xp, row(g_mix_pre), w_in_bf, 512)
    ohg, s_p = _hgrn_prompt(hin, lbp, gn, n_p, seq_len, 256, 16)
    osb = _sb_prompt(qn, kb, vb, bias, n_p, seq_len, 256)
    x1, qc = _mix_out(xp, ohg, osb, w_o_bf, row(g_mix_post), row(g_ca_pre), w_cq_bf, 512)
    mk_p, mv_p = _mem_kv(mem_prompt.reshape(n_p * N_MEM, D_MODEL), row(g_mem), w_ck_bf, w_cv_bf, 256)
    ca = _cross_prompt(qc, mk_p, mv_p, n_p, seq_len, 512)
    x2, h3 = _ca_out(ca, x1, w_co_bf, row(g_ca_post), row(g_ffn_pre), 512)
    yp, tail_p = _ffn_prompt(h3, x2, w_up_bf, cw, cb, w_dn_bf, row(g_ffn_post), n_p, seq_len, 256)

    ts = n_d * dec
    hin_s, sk_s, sv_s, qn_s, kb_s, vb_s = _in_proj(xs, row(g_mix_pre), w_in_bf, 512)
    ohg_s, s_s = _hgrn_sample(hin_s, lbp, gn, state_hgrn[l], dec, 16)
    n_pool = cache_sb_k.shape[1]
    osb_s = _sb_sample(qn_s, kb_s, vb_s, cache_sb_k[l].reshape(n_pool, PAGE, SB_W),
                       cache_sb_v[l].reshape(n_pool, PAGE, SB_W), page_table, bias, dec).reshape(ts, SB_W)
    x1_s, qc_s = _mix_out(xs, ohg_s, osb_s, w_o_bf, row(g_mix_post), row(g_ca_pre), w_cq_bf, 512)
    ca_s = _cross_sample(qc_s, cache_mem_k[l].reshape(n_d, N_MEM, D_MODEL),
                         cache_mem_v[l].reshape(n_d, N_MEM, D_MODEL), dec, 4)
    x2_s, h3_s = _ca_out(ca_s, x1_s, w_co_bf, row(g_ca_post), row(g_ffn_pre), 512)
    halo = jnp.pad(state_ffn_conv[l], ((0, 0), (0, dec - 2), (0, 0))).reshape(ts, 2 * D_FF)
    ys, u_s = _ffn_sample(h3_s, x2_s, halo, w_up_bf, cw, cb, w_dn_bf, row(g_ffn_post), dec, 128)

    return (yp.reshape(n_p, seq_len, D_MODEL), ys.reshape(n_d, dec, D_MODEL),
            sk.reshape(1, n_p, seq_len, SB_HEADS, SB_DH), sv.reshape(1, n_p, seq_len, SB_HEADS, SB_DH),
            s_p[None],
            tail_p.reshape(n_p, 8, 2 * D_FF)[None, :, 6:8],
            mk_p.reshape(1, n_p, N_MEM, CA_HEADS, CA_DH), mv_p.reshape(1, n_p, N_MEM, CA_HEADS, CA_DH),
            sk_s.reshape(1, n_d, dec, SB_HEADS, SB_DH), sv_s.reshape(1, n_d, dec, SB_HEADS, SB_DH),
            s_s[None],
            u_s.reshape(n_d, dec, 2 * D_FF)[None, :, dec - 2:dec])
```

```python
import functools

import numpy as np
import jax
import jax.numpy as jnp
from jax import lax
from jax.experimental import pallas as pl
from jax.experimental.pallas import tpu as pltpu

f32 = jnp.float32
bf16 = jnp.bfloat16

D_MODEL = 1024
HG_HEADS = 8
HG_DK = 64
HG_W = HG_HEADS * HG_DK
SB_HEADS = 8
SB_DH = 64
SB_W = SB_HEADS * SB_DH
SB_SCALE = SB_DH ** -0.5
LOG2E = 1.4426950408889634
N_MEM = 256
CA_HEADS = 4
CA_DH = D_MODEL // CA_HEADS
CA_SCALE = CA_DH ** -0.5
D_FF = 2816
RMS_EPS = 1e-6
PAGE = 128
LANES = 128
VMEM_LIMIT = 56 * 1024 * 1024


def _cparams(sem, flags=None):
    return pltpu.CompilerParams(dimension_semantics=sem, vmem_limit_bytes=VMEM_LIMIT, flags=flags)


def _const_spec(shape):
    nd = len(shape)
    return pl.BlockSpec(shape, lambda *_: (0,) * nd, pipeline_mode=pl.Buffered(1))


def _rms(x, g):
    return x * lax.rsqrt(jnp.mean(x * x, axis=-1, keepdims=True) + RMS_EPS) * g


def _dot(a, b):
    return jnp.dot(a, b, preferred_element_type=f32)


def _dot_nt(a, b):
    return lax.dot_general(a, b, (((1,), (1,)), ((), ())), preferred_element_type=f32)


def _dot_tn(a, b):
    return lax.dot_general(a, b, (((0,), (0,)), ((), ())), preferred_element_type=f32)


def _in_proj_body(x_ref, g_ref, w_ref, hin_ref, sk_ref, sv_ref, qn_ref, kb_ref, vb_ref):
    xn = _rms(x_ref[...], g_ref[...]).astype(bf16)
    for j in range(4):
        hin_ref[:, j * HG_W:(j + 1) * HG_W] = _dot(xn, w_ref[:, j * HG_W:(j + 1) * HG_W])
    base = 4 * HG_W
    q = _dot(xn, w_ref[:, base:base + SB_W])
    qn_ref[...] = (q * (-SB_SCALE * LOG2E)).astype(bf16)
    k = _dot(xn, w_ref[:, base + SB_W:base + 2 * SB_W])
    sk_ref[...] = k
    kb_ref[...] = k.astype(bf16)
    v = _dot(xn, w_ref[:, base + 2 * SB_W:base + 3 * SB_W])
    sv_ref[...] = v
    vb_ref[...] = v.astype(bf16)


def _in_proj(x, g, w_bf, tm):
    t = x.shape[0]
    d_in = w_bf.shape[1]
    row = lambda n: pl.BlockSpec((tm, n), lambda i: (i, 0))
    return pl.pallas_call(
        _in_proj_body,
        grid=(t // tm,),
        in_specs=[row(D_MODEL), _const_spec((1, D_MODEL)), _const_spec((D_MODEL, d_in))],
        out_specs=[row(4 * HG_W), row(SB_W), row(SB_W), row(SB_W), row(SB_W), row(SB_W)],
        out_shape=[jax.ShapeDtypeStruct((t, 4 * HG_W), f32),
                   jax.ShapeDtypeStruct((t, SB_W), f32), jax.ShapeDtypeStruct((t, SB_W), f32),
                   jax.ShapeDtypeStruct((t, SB_W), bf16), jax.ShapeDtypeStruct((t, SB_W), bf16),
                   jax.ShapeDtypeStruct((t, SB_W), bf16)],
        compiler_params=_cparams(("arbitrary",)),
        name="in_proj",
    )(x, g, w_bf)


def _hgrn_consts(nb, c):
    r = np.arange(nb)
    same = (r[:, None] // c) == (r[None, :] // c)
    tri = same & (r[None, :] <= r[:, None])
    stack = np.concatenate([tri, same], axis=0).astype(np.float32)
    h = np.arange(HG_W) // HG_DK
    bo = (h[:, None] == h[None, :]).astype(np.float32)
    return jnp.asarray(stack, bf16), jnp.asarray(bo, bf16)


def _hgrn_prep(hin_ref, lbp_ref, stack_ref, scr, nb):
    q_scr, k_scr, v_scr, b_scr, qe_scr, kt_scr, el_scr = scr
    a = lbp_ref[...]
    e = jnp.exp(a - jnp.max(a, axis=0, keepdims=True))
    lb = e[0:1] / jnp.sum(e, axis=0, keepdims=True)
    hq = hin_ref[:, 0:HG_W]
    f = lb + (1.0 - lb) * jax.nn.sigmoid(hin_ref[:, HG_W:2 * HG_W])
    g = jnp.log(f)
    kk = 1.0 - f
    g1 = g.astype(bf16)
    r1 = g - g1.astype(f32)
    g2 = r1.astype(bf16)
    g3 = (r1 - g2.astype(f32)).astype(bf16)
    st = stack_ref[...]
    bb = _dot(st, g1) + _dot(st, g2) + _dot(st, g3)
    b = bb[:nb]
    btot = bb[nb:]
    q_scr[...] = hq
    k_scr[...] = kk
    v_scr[...] = hin_ref[:, 2 * HG_W:3 * HG_W]
    b_scr[...] = b
    qe_scr[...] = hq * jnp.exp(b)
    kt_scr[...] = kk * jnp.exp(btot - b)
    el_scr[...] = jnp.exp(btot)


def _hgrn_chunk(scr, bo, r0, c, st_list):
    q_scr, k_scr, v_scr, b_scr, qe_scr, kt_scr, el_scr = scr
    rows = pl.ds(r0, c)
    b_c = b_scr[rows, :]
    q_c = q_scr[rows, :]
    k_c = k_scr[rows, :]
    v_c = v_scr[rows, :]
    qe_c = qe_scr[rows, :].astype(bf16)
    kt_c = kt_scr[rows, :].astype(bf16)
    el_c = el_scr[pl.ds(r0, 1), :]
    v_cb = v_c.astype(bf16)
    t_idx = lax.broadcasted_iota(jnp.int32, (c, 1), 0)
    slabs = []
    for s in range(c):
        d = jnp.exp(jnp.minimum(b_c - b_c[s:s + 1, :], 0.0))
        slabs.append(jnp.where(t_idx >= s, q_c * d * k_c[s:s + 1, :], 0.0))
    p = jnp.concatenate(slabs, axis=0).astype(bf16)
    pw = _dot(p, bo)
    o = pw[0:c, :] * v_c[0:1, :]
    for s in range(1, c):
        o = o + pw[s * c:(s + 1) * c, :] * v_c[s:s + 1, :]
    ri = lax.broadcasted_iota(jnp.int32, (LANES, LANES), 0) // HG_DK
    ci = lax.broadcasted_iota(jnp.int32, (LANES, LANES), 1) // HG_DK
    same_head = ri == ci
    o_parts, new_states = [], []
    for pr in range(HG_W // LANES):
        sl = slice(pr * LANES, (pr + 1) * LANES)
        st = st_list[pr]
        o_parts.append(_dot_nt(qe_c[:, sl], st.astype(bf16)))
        upd = _dot_tn(v_cb[:, sl], kt_c[:, sl])
        new_states.append(st * el_c[:, sl] + jnp.where(same_head, upd, 0.0))
    return o + jnp.concatenate(o_parts, axis=1), new_states


def _hgrn_finish(o, hin_ref, gn_ref, bo, o_ref):
    ms = _dot((o * o).astype(bf16), bo) * (1.0 / HG_DK)
    gate = hin_ref[:, 3 * HG_W:4 * HG_W]
    o_ref[...] = (o * lax.rsqrt(ms + RMS_EPS) * gn_ref[...] * (gate * jax.nn.sigmoid(gate))).astype(o_ref.dtype)


def _pair_state_out(st):
    t = st.T
    return t[0:HG_DK, 0:HG_DK], t[HG_DK:LANES, HG_DK:LANES]


def _hgrn_prompt_body(hin_ref, lbp_ref, gn_ref, stack_ref, bo_ref, o_ref, sout_ref,
                      st_scr, oacc_scr, *scr, nb, c):
    j = pl.program_id(1)

    @pl.when(j == 0)
    def _():
        st_scr[...] = jnp.zeros_like(st_scr)

    _hgrn_prep(hin_ref, lbp_ref, stack_ref, scr, nb)
    bo = bo_ref[...]

    def step(ci, carry):
        r0 = pl.multiple_of(ci * c, c)
        o, new_states = _hgrn_chunk(scr, bo, r0, c, [st_scr[pr] for pr in range(HG_W // LANES)])
        for pr, st in enumerate(new_states):
            st_scr[pr] = st
        oacc_scr[pl.ds(r0, c), :] = o
        return carry

    lax.fori_loop(0, nb // c, step, 0)
    _hgrn_finish(oacc_scr[...], hin_ref, gn_ref, bo, o_ref)

    @pl.when(j == pl.num_programs(1) - 1)
    def _():
        for pr in range(HG_W // LANES):
            sa, sb = _pair_state_out(st_scr[pr])
            sout_ref[0, 2 * pr] = sa
            sout_ref[0, 2 * pr + 1] = sb


def _hgrn_prompt(hin, lbp, gn, n_seq, seq_len, nb, c):
    stack, bo = _hgrn_consts(nb, c)
    nblk = seq_len // nb
    scr = [pltpu.VMEM((nb, HG_W), f32) for _ in range(7)]
    return pl.pallas_call(
        functools.partial(_hgrn_prompt_body, nb=nb, c=c),
        grid=(n_seq, nblk),
        in_specs=[pl.BlockSpec((nb, 4 * HG_W), lambda b, j: (b * nblk + j, 0)),
                  _const_spec(lbp.shape), _const_spec((1, HG_W)),
                  _const_spec(stack.shape), _const_spec(bo.shape)],
        out_specs=[pl.BlockSpec((nb, HG_W), lambda b, j: (b * nblk + j, 0)),
                   pl.BlockSpec((1, HG_HEADS, HG_DK, HG_DK), lambda b, j: (b, 0, 0, 0))],
        out_shape=[jax.ShapeDtypeStruct((n_seq * seq_len, HG_W), bf16),
                   jax.ShapeDtypeStruct((n_seq, HG_HEADS, HG_DK, HG_DK), f32)],
        scratch_shapes=[pltpu.VMEM((HG_W // LANES, LANES, LANES), f32), pltpu.VMEM((nb, HG_W), f32)] + scr,
        compiler_params=_cparams(("arbitrary", "arbitrary")),
        name="hgrn_prompt",
    )(hin, lbp, gn, stack, bo)


def _hgrn_sample_body(hin_ref, lbp_ref, gn_ref, stack_ref, bo_ref, s0_ref, o_ref, sout_ref,
                      oacc_scr, *scr, nb, c):
    _hgrn_prep(hin_ref, lbp_ref, stack_ref, scr, nb)
    bo = bo_ref[...]
    zero = jnp.zeros((HG_DK, HG_DK), f32)

    def step(ci, carry):
        r0 = pl.multiple_of(ci * c, c)
        states = []
        for pr in range(HG_W // LANES):
            sa = s0_ref[ci, 2 * pr]
            sb = s0_ref[ci, 2 * pr + 1]
            bd = jnp.concatenate([jnp.concatenate([sa, zero], axis=1),
                                  jnp.concatenate([zero, sb], axis=1)], axis=0)
            states.append(bd.T)
        o, new_states = _hgrn_chunk(scr, bo, r0, c, states)
        for pr, st in enumerate(new_states):
            sa, sb = _pair_state_out(st)
            sout_ref[ci, 2 * pr] = sa
            sout_ref[ci, 2 * pr + 1] = sb
        oacc_scr[pl.ds(r0, c), :] = o
        return carry

    lax.fori_loop(0, nb // c, step, 0)
    _hgrn_finish(oacc_scr[...], hin_ref, gn_ref, bo, o_ref)


def _hgrn_sample(hin, lbp, gn, s0, c, seqs_per_step):
    n_seq = s0.shape[0]
    nb = seqs_per_step * c
    stack, bo = _hgrn_consts(nb, c)
    scr = [pltpu.VMEM((nb, HG_W), f32) for _ in range(7)]
    st_spec = pl.BlockSpec((seqs_per_step, HG_HEADS, HG_DK, HG_DK), lambda i: (i, 0, 0, 0))
    return pl.pallas_call(
        functools.partial(_hgrn_sample_body, nb=nb, c=c),
        grid=(n_seq // seqs_per_step,),
        in_specs=[pl.BlockSpec((nb, 4 * HG_W), lambda i: (i, 0)),
                  _const_spec(lbp.shape), _const_spec((1, HG_W)),
                  _const_spec(stack.shape), _const_spec(bo.shape), st_spec],
        out_specs=[pl.BlockSpec((nb, HG_W), lambda i: (i, 0)), st_spec],
        out_shape=[jax.ShapeDtypeStruct((n_seq * c, HG_W), bf16),
                   jax.ShapeDtypeStruct(s0.shape, f32)],
        scratch_shapes=[pltpu.VMEM((nb, HG_W), f32)] + scr,
        compiler_params=_cparams(("arbitrary",)),
        name="hgrn_sample",
    )(hin, lbp, gn, stack, bo, s0)


NEG_BIG = -1e30


def _sb_logs(zp_raw, nbias, r, strict):
    zp = zp_raw + nbias
    neg_abs = lax.bitcast_convert_type(lax.bitcast_convert_type(zp, jnp.uint32) | jnp.uint32(0x80000000), f32)
    c = jnp.minimum(zp, 0.0) - jnp.log2(1.0 + jnp.exp2(neg_abs))
    ls = (c - zp) + r
    if strict is not None:
        c = jnp.where(strict, c, 0.0)
        ls = jnp.where(strict, ls, NEG_BIG)
    return c.astype(bf16), ls, r + jnp.sum(c, axis=-1, keepdims=True)


def _sb_weights(ls, later):
    return jnp.exp2(ls + later).astype(bf16)


def _sb_prompt_body(bias_ref, q_ref, k_ref, v_ref, u_ref, o_ref,
                    qs_scr, zp_scr, c_scr, ls_scr, loc_scr, a_scr, acc_scr, r_scr, *, tq, tk):
    pr = pl.program_id(1)
    i = pl.program_id(2)
    nk = pl.num_programs(2) * (tq // tk)
    lane = lax.broadcasted_iota(jnp.int32, (1, LANES), 1)
    q = q_ref[...]
    zero_q = jnp.zeros_like(q)
    qs_scr[0:tq, :] = jnp.where(lane < SB_DH, q, zero_q)
    qs_scr[tq:2 * tq, :] = jnp.where(lane >= SB_DH, q, zero_q)
    row = lax.broadcasted_iota(jnp.int32, (2 * tq, 1), 0)
    nbias = jnp.where(row < tq, -LOG2E * bias_ref[2 * pr], -LOG2E * bias_ref[2 * pr + 1])
    t_row = lax.broadcasted_iota(jnp.int32, (2 * tq, tk), 0) % tq
    s_col = lax.broadcasted_iota(jnp.int32, (2 * tq, tk), 1)
    top = 2 * i + 1

    def kblock(ref, kb):
        return ref[pl.ds(pl.multiple_of(kb * tk, tk), tk), :]

    def scores(kb):
        return _dot_nt(qs_scr[...], kblock(k_ref, jnp.maximum(kb, 0)))

    c0, ls0, r0 = _sb_logs(scores(top), nbias, jnp.zeros((2 * tq, 1), f32), s_col + tk < t_row)
    c_scr[0] = c0
    ls_scr[0] = ls0
    r_scr[...] = r0
    zp_scr[1] = scores(top - 1)
    a_scr[...] = jnp.zeros_like(a_scr)
    acc_scr[...] = jnp.zeros_like(acc_scr)

    def step(n, slot, strict):
        prev = 1 - slot
        acc_scr[...] += _dot(a_scr[...], kblock(v_ref, jnp.minimum(top - n + 2, nk - 1)))
        loc_scr[...] = _dot(c_scr[prev], u_ref[...])
        zp_scr[prev] = scores(top - n - 1)
        c, ls, r = _sb_logs(zp_scr[slot], nbias, r_scr[...], strict)
        c_scr[slot] = c
        ls_scr[slot] = ls
        r_scr[...] = r
        a_scr[...] = _sb_weights(ls_scr[prev], loc_scr[...])

    step(1, 1, s_col < t_row)

    def pair(j, carry):
        step(2 * j + 2, 0, None)
        step(2 * j + 3, 1, None)
        return carry

    lax.fori_loop(0, i, pair, 0)
    acc = acc_scr[...] + _dot(a_scr[...], kblock(v_ref, 1))
    a_last = _sb_weights(ls_scr[1], _dot(c_scr[1], u_ref[...]))
    acc = acc + _dot(a_last, kblock(v_ref, 0))
    o_ref[...] = jnp.where(lane < SB_DH, acc[:tq], acc[tq:]).astype(o_ref.dtype)


def _strict_upper(n):
    r = np.arange(n)
    return jnp.asarray((r[:, None] > r[None, :]).astype(np.float32), bf16)


def _sb_prompt(qn, kb, vb, bias, n_seq, seq_len, tk):
    tq = 2 * tk
    nq = seq_len // tq
    u = _strict_upper(tk)
    return pl.pallas_call(
        functools.partial(_sb_prompt_body, tq=tq, tk=tk),
        grid=(n_seq, SB_W // LANES, nq),
        in_specs=[pl.BlockSpec(memory_space=pltpu.SMEM),
                  pl.BlockSpec((tq, LANES), lambda b, p, i: (b * nq + i, p)),
                  pl.BlockSpec((seq_len, LANES), lambda b, p, i: (b, p)),
                  pl.BlockSpec((seq_len, LANES), lambda b, p, i: (b, p)),
                  _const_spec(u.shape)],
        out_specs=pl.BlockSpec((tq, LANES), lambda b, p, i: (b * nq + i, p)),
        out_shape=jax.ShapeDtypeStruct((n_seq * seq_len, SB_W), bf16),
        scratch_shapes=[pltpu.VMEM((2 * tq, LANES), bf16),
                        pltpu.VMEM((2, 2 * tq, tk), f32), pltpu.VMEM((2, 2 * tq, tk), bf16),
                        pltpu.VMEM((2, 2 * tq, tk), f32), pltpu.VMEM((2 * tq, tk), f32),
                        pltpu.VMEM((2 * tq, tk), bf16), pltpu.VMEM((2 * tq, LANES), f32),
                        pltpu.VMEM((2 * tq, 1), f32)],
        compiler_params=_cparams(("arbitrary", "arbitrary", "arbitrary")),
        name="sb_prompt",
    )(bias, qn, kb, vb, u)


def _sb_sample_body(pt_ref, bias_ref, q_ref, kn_ref, vn_ref, u_ref, *rest, n_pages, dec):
    k_refs = rest[:n_pages]
    v_refs = rest[n_pages:2 * n_pages]
    o_ref = rest[2 * n_pages]
    rows = SB_HEADS * dec
    lane_head = lax.broadcasted_iota(jnp.int32, (1, SB_W), 1) // SB_DH
    q = q_ref[0].astype(f32)
    qbd = jnp.concatenate([jnp.where(lane_head == h, q, 0.0) for h in range(SB_HEADS)], axis=0).astype(bf16)
    row_head = lax.broadcasted_iota(jnp.int32, (rows, 1), 0) // dec
    nbias = jnp.zeros((rows, 1), f32)
    for h in range(SB_HEADS):
        nbias = jnp.where(row_head == h, -LOG2E * bias_ref[h], nbias)
    u = u_ref[...]
    pad = jnp.zeros((PAGE - dec, SB_W), f32)
    kn = jnp.concatenate([kn_ref[0].astype(f32), pad], axis=0).astype(bf16)
    vn = jnp.concatenate([vn_ref[0].astype(f32), pad], axis=0).astype(bf16)
    t_row = lax.broadcasted_iota(jnp.int32, (rows, PAGE), 0) % dec
    s_col = lax.broadcasted_iota(jnp.int32, (rows, PAGE), 1)
    c, ls, r = _sb_logs(_dot_nt(qbd, kn), nbias, jnp.zeros((rows, 1), f32), s_col < t_row)
    acc = _dot(_sb_weights(ls, _dot(c, u)), vn)

    def page(ref):
        return jnp.concatenate([ref[0, :, h, :] for h in range(SB_HEADS)], axis=1).astype(bf16)

    for j in range(n_pages - 1, -1, -1):
        c, ls, r = _sb_logs(_dot_nt(qbd, page(k_refs[j])), nbias, r, None)
        acc = acc + _dot(_sb_weights(ls, _dot(c, u)), page(v_refs[j]))
    out = jnp.zeros((dec, SB_W), f32)
    for h in range(SB_HEADS):
        out = out + jnp.where(lane_head == h, acc[h * dec:(h + 1) * dec, :], 0.0)
    o_ref[0] = out.astype(o_ref.dtype)


def _sb_sample(qn, kb, vb, cache_k, cache_v, page_table, bias, dec):
    n_seq, n_pages = page_table.shape
    u = _strict_upper(PAGE)
    pt = page_table.reshape(-1)
    tok = pl.BlockSpec((1, dec, SB_W), lambda n, pt: (n, 0, 0))

    def page_spec(j):
        return pl.BlockSpec((1, PAGE, SB_HEADS, SB_DH), lambda n, pt: (pt[n * n_pages + j], 0, 0, 0))

    grid_spec = pltpu.PrefetchScalarGridSpec(
        num_scalar_prefetch=1,
        grid=(n_seq,),
        in_specs=[pl.BlockSpec(memory_space=pltpu.SMEM), tok, tok, tok,
                  pl.BlockSpec(u.shape, lambda n, pt: (0, 0))]
                 + [page_spec(j) for j in range(n_pages)] * 2,
        out_specs=tok,
    )
    return pl.pallas_call(
        functools.partial(_sb_sample_body, n_pages=n_pages, dec=dec),
        grid_spec=grid_spec,
        out_shape=jax.ShapeDtypeStruct((n_seq, dec, SB_W), bf16),
        compiler_params=_cparams(("arbitrary",)),
        name="sb_sample",
    )(pt, bias, qn.reshape(n_seq, dec, SB_W), kb.reshape(n_seq, dec, SB_W), vb.reshape(n_seq, dec, SB_W), u,
      *([cache_k] * n_pages), *([cache_v] * n_pages))


def _mix_out_body(x_ref, ohg_ref, osb_ref, wo_ref, g1_ref, g2_ref, wq_ref, x1_ref, qc_ref):
    mixed = _dot(ohg_ref[...], wo_ref[0:HG_W, :]) + _dot(osb_ref[...], wo_ref[HG_W:HG_W + SB_W, :])
    x1 = x_ref[...] + _rms(mixed, g1_ref[...])
    x1_ref[...] = x1
    h2 = _rms(x1, g2_ref[...]).astype(bf16)
    qc_ref[...] = (_dot(h2, wq_ref[...]) * CA_SCALE).astype(bf16)


def _mix_out(x, ohg, osb, wo_bf, g1, g2, wq_bf, tm):
    t = x.shape[0]
    row = lambda n: pl.BlockSpec((tm, n), lambda i: (i, 0))
    return pl.pallas_call(
        _mix_out_body,
        grid=(t // tm,),
        in_specs=[row(D_MODEL), row(HG_W), row(SB_W), _const_spec(wo_bf.shape),
                  _const_spec((1, D_MODEL)), _const_spec((1, D_MODEL)), _const_spec(wq_bf.shape)],
        out_specs=[row(D_MODEL), row(D_MODEL)],
        out_shape=[jax.ShapeDtypeStruct((t, D_MODEL), f32), jax.ShapeDtypeStruct((t, D_MODEL), bf16)],
        compiler_params=_cparams(("arbitrary",)),
        name="mix_out",
    )(x, ohg, osb, wo_bf, g1, g2, wq_bf)


def _mem_kv_body(m_ref, g_ref, wk_ref, wv_ref, mk_ref, mv_ref):
    mn = _rms(m_ref[...], g_ref[...]).astype(bf16)
    mk_ref[...] = _dot(mn, wk_ref[...])
    mv_ref[...] = _dot(mn, wv_ref[...])


def _mem_kv(mem, g, wk_bf, wv_bf, tm):
    t = mem.shape[0]
    row = pl.BlockSpec((tm, D_MODEL), lambda i: (i, 0))
    return pl.pallas_call(
        _mem_kv_body,
        grid=(t // tm,),
        in_specs=[row, _const_spec((1, D_MODEL)), _const_spec(wk_bf.shape), _const_spec(wv_bf.shape)],
        out_specs=[row, row],
        out_shape=[jax.ShapeDtypeStruct((t, D_MODEL), f32)] * 2,
        compiler_params=_cparams(("arbitrary",)),
        name="mem_kv",
    )(mem, g, wk_bf, wv_bf)


def _cross_heads(q, mk_head, mv_head):
    outs = []
    for h in range(CA_HEADS):
        s = _dot_nt(q[:, h * CA_DH:(h + 1) * CA_DH], mk_head(h))
        s = s - jnp.max(s, axis=-1, keepdims=True)
        e = jnp.exp(s)
        p = e / jnp.sum(e, axis=-1, keepdims=True)
        outs.append(_dot(p.astype(bf16), mv_head(h)))
    return jnp.concatenate(outs, axis=1)


def _cross_prompt_body(q_ref, mk_ref, mv_ref, o_ref):
    mk = mk_ref[...].astype(bf16)
    mv = mv_ref[...].astype(bf16)
    o_ref[...] = _cross_heads(q_ref[...], lambda h: mk[:, h * CA_DH:(h + 1) * CA_DH],
                              lambda h: mv[:, h * CA_DH:(h + 1) * CA_DH]).astype(o_ref.dtype)


def _cross_prompt(qc, mk, mv, n_seq, seq_len, tm):
    nblk = seq_len // tm
    row = pl.BlockSpec((tm, D_MODEL), lambda b, j: (b * nblk + j, 0))
    mem = pl.BlockSpec((N_MEM, D_MODEL), lambda b, j: (b, 0))
    return pl.pallas_call(
        _cross_prompt_body,
        grid=(n_seq, nblk),
        in_specs=[row, mem, mem],
        out_specs=row,
        out_shape=jax.ShapeDtypeStruct(qc.shape, bf16),
        compiler_params=_cparams(("arbitrary", "arbitrary")),
        name="cross_prompt",
    )(qc, mk, mv)


def _cross_sample_body(q_ref, mk_ref, mv_ref, o_ref, *, group, dec):
    for s in range(group):
        o_ref[s * dec:(s + 1) * dec, :] = _cross_heads(
            q_ref[s * dec:(s + 1) * dec, :], lambda h: mk_ref[s, :, h, :].astype(bf16),
            lambda h: mv_ref[s, :, h, :].astype(bf16)).astype(o_ref.dtype)


def _cross_sample(qc, mk, mv, dec, group):
    n_seq = mk.shape[0]
    row = pl.BlockSpec((group * dec, D_MODEL), lambda i: (i, 0))
    mem = pl.BlockSpec((group, N_MEM, CA_HEADS, CA_DH), lambda i: (i, 0, 0, 0))
    return pl.pallas_call(
        functools.partial(_cross_sample_body, group=group, dec=dec),
        grid=(n_seq // group,),
        in_specs=[row, mem, mem],
        out_specs=row,
        out_shape=jax.ShapeDtypeStruct(qc.shape, bf16),
        compiler_params=_cparams(("arbitrary",)),
        name="cross_sample",
    )(qc, mk, mv)


def _ca_out_body(ca_ref, x1_ref, wco_ref, g1_ref, g2_ref, x2_ref, h3_ref):
    x2 = x1_ref[...] + _rms(_dot(ca_ref[...], wco_ref[...]), g1_ref[...])
    x2_ref[...] = x2
    h3_ref[...] = _rms(x2, g2_ref[...]).astype(bf16)


def _ca_out(ca, x1, wco_bf, g1, g2, tm):
    t = x1.shape[0]
    row = pl.BlockSpec((tm, D_MODEL), lambda i: (i, 0))
    return pl.pallas_call(
        _ca_out_body,
        grid=(t // tm,),
        in_specs=[row, row, _const_spec(wco_bf.shape), _const_spec((1, D_MODEL)), _const_spec((1, D_MODEL))],
        out_specs=[row, row],
        out_shape=[jax.ShapeDtypeStruct((t, D_MODEL), f32), jax.ShapeDtypeStruct((t, D_MODEL), bf16)],
        compiler_params=_cparams(("arbitrary",)),
        name="ca_out",
    )(ca, x1, wco_bf, g1, g2)


FFN_CHUNK = 256


def _ffn_body(*refs, tm, dec, carry_tail):
    if carry_tail:
        h3_ref, x2_ref, wup_ref, cw_ref, cb_ref, wdn_ref, g_ref, y_ref, tail_ref, tail_scr, act_scr = refs
        halo_ref = None

        @pl.when(pl.program_id(1) == 0)
        def _():
            tail_scr[...] = jnp.zeros_like(tail_scr)
    else:
        h3_ref, x2_ref, halo_ref, wup_ref, cw_ref, cb_ref, wdn_ref, g_ref, y_ref, u_ref, act_scr = refs
        pos = lax.broadcasted_iota(jnp.int32, (tm, 1), 0) % dec
    h3 = h3_ref[...]

    def conv_chunk(col):
        cs = slice(col, col + FFN_CHUNK)
        u = _dot(h3, wup_ref[:, cs])
        if carry_tail:
            ext = jnp.concatenate([tail_scr[:, cs], u], axis=0)
            u1 = ext[7:7 + tm]
            u2 = ext[6:6 + tm]
            tail_scr[:, cs] = u[tm - 8:tm]
            tail_ref[:, cs] = u[tm - 8:tm]
        else:
            ext = jnp.concatenate([jnp.zeros((8, FFN_CHUNK), f32), u], axis=0)
            hal = halo_ref[:, cs]
            hext = jnp.concatenate([hal, jnp.zeros((8, FFN_CHUNK), f32)], axis=0)
            u1 = jnp.where(pos == 0, hext[1:1 + tm], ext[7:7 + tm])
            u2 = jnp.where(pos < 2, hal, ext[6:6 + tm])
            u_ref[:, cs] = u
        return cb_ref[:, cs] + cw_ref[2:3, cs] * u + cw_ref[1:2, cs] * u1 + cw_ref[0:1, cs] * u2

    for cj in range(D_FF // FFN_CHUNK):
        gate = conv_chunk(cj * FFN_CHUNK)
        val = conv_chunk(D_FF + cj * FFN_CHUNK)
        act_scr[:, cj * FFN_CHUNK:(cj + 1) * FFN_CHUNK] = (jax.nn.gelu(gate, approximate=True) * val).astype(bf16)
    y = _dot(act_scr[...], wdn_ref[...])
    y_ref[...] = x2_ref[...] + _rms(y, g_ref[...])


def _ffn_prompt(h3, x2, wup_bf, cw, cb, wdn_bf, g, n_seq, seq_len, tm):
    nblk = seq_len // tm
    row = pl.BlockSpec((tm, D_MODEL), lambda b, j: (b * nblk + j, 0))
    return pl.pallas_call(
        functools.partial(_ffn_body, tm=tm, dec=None, carry_tail=True),
        grid=(n_seq, nblk),
        in_specs=[row, row, _const_spec(wup_bf.shape), _const_spec(cw.shape), _const_spec(cb.shape),
                  _const_spec(wdn_bf.shape), _const_spec((1, D_MODEL))],
        out_specs=[row, pl.BlockSpec((8, 2 * D_FF), lambda b, j: (b, 0))],
        out_shape=[jax.ShapeDtypeStruct((n_seq * seq_len, D_MODEL), f32),
                   jax.ShapeDtypeStruct((n_seq * 8, 2 * D_FF), f32)],
        scratch_shapes=[pltpu.VMEM((8, 2 * D_FF), f32), pltpu.VMEM((tm, D_FF), bf16)],
        compiler_params=_cparams(("arbitrary", "arbitrary")),
        name="ffn_prompt",
    )(h3, x2, wup_bf, cw, cb, wdn_bf, g)


def _ffn_sample(h3, x2, halo, wup_bf, cw, cb, wdn_bf, g, dec, tm):
    t = h3.shape[0]
    row = pl.BlockSpec((tm, D_MODEL), lambda i: (i, 0))
    wide = pl.BlockSpec((tm, 2 * D_FF), lambda i: (i, 0))
    return pl.pallas_call(
        functools.partial(_ffn_body, tm=tm, dec=dec, carry_tail=False),
        grid=(t // tm,),
        in_specs=[row, row, wide, _const_spec(wup_bf.shape), _const_spec(cw.shape), _const_spec(cb.shape),
                  _const_spec(wdn_bf.shape), _const_spec((1, D_MODEL))],
        out_specs=[row, wide],
        out_shape=[jax.ShapeDtypeStruct((t, D_MODEL), f32), jax.ShapeDtypeStruct((t, 2 * D_FF), f32)],
        scratch_shapes=[pltpu.VMEM((tm, D_FF), bf16)],
        compiler_params=_cparams(("arbitrary",)),
        name="ffn_sample",
    )(h3, x2, halo, wup_bf, cw, cb, wdn_bf, g)


def kernel(x_prompt, x_sample, cache_sb_k, cache_sb_v, state_hgrn, state_ffn_conv, cache_mem_k, cache_mem_v,
           page_table, mem_prompt, w_in, hg_norm, hg_lb, sb_bias, w_o, g_mix_pre, g_mix_post, g_ca_pre, g_ca_post,
           g_mem, w_cq, w_ck, w_cv, w_co, g_ffn_pre, g_ffn_post, w_up, conv_w, conv_b, w_down):
    n_p, seq_len, _ = x_prompt.shape
    n_d, dec, _ = x_sample.shape
    depth = w_in.shape[0]
    assert depth == 1, "single-layer step"
    assert dec >= 2 and dec % 8 == 0, "the conv tail is taken from the new rows"
    l = 0
    row = lambda a: a[l].reshape(1, -1)
    w_in_bf, w_o_bf, w_cq_bf, w_co_bf = (w[l].astype(bf16) for w in (w_in, w_o, w_cq, w_co))
    w_ck_bf, w_cv_bf, w_up_bf, w_dn_bf = (w[l].astype(bf16) for w in (w_ck, w_cv, w_up, w_down))
    lbp = hg_lb[l:l + 2]
    gn = row(hg_norm)
    bias = sb_bias[l]
    cb = row(conv_b)
    cw = conv_w[l]

    xp = x_prompt.reshape(n_p * seq_len, D_MODEL)
    xs = x_sample.reshape(n_d * dec, D_MODEL)

    hin, sk, sv, qn, kb, vb = _in_proj(xp, row(g_mix_pre), w_in_bf, 512)
    ohg, s_p = _hgrn_prompt(hin, lbp, gn, n_p, seq_len, 256, 16)
    osb = _sb_prompt(qn, kb, vb, bias, n_p, seq_len, 256)
    x1, qc = _mix_out(xp, ohg, osb, w_o_bf, row(g_mix_post), row(g_ca_pre), w_cq_bf, 512)
    mk_p, mv_p = _mem_kv(mem_prompt.reshape(n_p * N_MEM, D_MODEL), row(g_mem), w_ck_bf, w_cv_bf, 256)
    ca = _cross_prompt(qc, mk_p, mv_p, n_p, seq_len, 512)
    x2, h3 = _ca_out(ca, x1, w_co_bf, row(g_ca_post), row(g_ffn_pre), 512)
    yp, tail_p = _ffn_prompt(h3, x2, w_up_bf, cw, cb, w_dn_bf, row(g_ffn_post), n_p, seq_len, 256)

    ts = n_d * dec
    hin_s, sk_s, sv_s, qn_s, kb_s, vb_s = _in_proj(xs, row(g_mix_pre), w_in_bf, 512)
    ohg_s, s_s = _hgrn_sample(hin_s, lbp, gn, state_hgrn[l], dec, 16)
    osb_s = _sb_sample(qn_s, kb_s, vb_s, cache_sb_k[l], cache_sb_v[l], page_table, bias, dec).reshape(ts, SB_W)
    x1_s, qc_s = _mix_out(xs, ohg_s, osb_s, w_o_bf, row(g_mix_post), row(g_ca_pre), w_cq_bf, 512)
    ca_s = _cross_sample(qc_s, cache_mem_k[l], cache_mem_v[l], dec, 2)
    x2_s, h3_s = _ca_out(ca_s, x1_s, w_co_bf, row(g_ca_post), row(g_ffn_pre), 512)
    halo = jnp.pad(state_ffn_conv[l], ((0, 0), (0, dec - 2), (0, 0))).reshape(ts, 2 * D_FF)
    ys, u_s = _ffn_sample(h3_s, x2_s, halo, w_up_bf, cw, cb, w_dn_bf, row(g_ffn_post), dec, 128)

    return (yp.reshape(n_p, seq_len, D_MODEL), ys.reshape(n_d, dec, D_MODEL),
            sk.reshape(1, n_p, seq_len, SB_HEADS, SB_DH), sv.reshape(1, n_p, seq_len, SB_HEADS, SB_DH),
            s_p[None],
            tail_p.reshape(n_p, 8, 2 * D_FF)[None, :, 6:8],
            mk_p.reshape(1, n_p, N_MEM, CA_HEADS, CA_DH), mv_p.reshape(1, n_p, N_MEM, CA_HEADS, CA_DH),
            sk_s.reshape(1, n_d, dec, SB_HEADS, SB_DH), sv_s.reshape(1, n_d, dec, SB_HEADS, SB_DH),
            s_s[None],
            u_s.reshape(n_d, dec, 2 * D_FF)[None, :, dec - 2:dec])
```

```python
import functools

import numpy as np
import jax
import jax.numpy as jnp
from jax import lax
from jax.experimental import pallas as pl
from jax.experimental.pallas import tpu as pltpu

f32 = jnp.float32
bf16 = jnp.bfloat16

D_MODEL = 1024
HG_HEADS = 8
HG_DK = 64
HG_W = HG_HEADS * HG_DK
SB_HEADS = 8
SB_DH = 64
SB_W = SB_HEADS * SB_DH
SB_SCALE = SB_DH ** -0.5
LOG2E = 1.4426950408889634
N_MEM = 256
CA_HEADS = 4
CA_DH = D_MODEL // CA_HEADS
CA_SCALE = CA_DH ** -0.5
D_FF = 2816
RMS_EPS = 1e-6
PAGE = 128
LANES = 128
VMEM_LIMIT = 56 * 1024 * 1024


def _cparams(sem, flags=None):
    return pltpu.CompilerParams(dimension_semantics=sem, vmem_limit_bytes=VMEM_LIMIT, flags=flags)


def _const_spec(shape):
    nd = len(shape)
    return pl.BlockSpec(shape, lambda *_: (0,) * nd, pipeline_mode=pl.Buffered(1))


def _rms(x, g):
    return x * lax.rsqrt(jnp.mean(x * x, axis=-1, keepdims=True) + RMS_EPS) * g


def _dot(a, b):
    return jnp.dot(a, b, preferred_element_type=f32)


def _dot_nt(a, b):
    return lax.dot_general(a, b, (((1,), (1,)), ((), ())), preferred_element_type=f32)


def _dot_tn(a, b):
    return lax.dot_general(a, b, (((0,), (0,)), ((), ())), preferred_element_type=f32)


def _in_proj_body(x_ref, g_ref, w_ref, hin_ref, sk_ref, sv_ref, qn_ref, kb_ref, vb_ref, *, kv_transposed):
    xn = _rms(x_ref[...], g_ref[...]).astype(bf16)
    for j in range(4):
        hin_ref[:, j * HG_W:(j + 1) * HG_W] = _dot(xn, w_ref[:, j * HG_W:(j + 1) * HG_W])
    base = 4 * HG_W
    q = _dot(xn, w_ref[:, base:base + SB_W])
    qn_ref[...] = (q * (-SB_SCALE * LOG2E)).astype(bf16)
    k = _dot(xn, w_ref[:, base + SB_W:base + 2 * SB_W])
    kb_ref[...] = k.astype(bf16)
    v = _dot(xn, w_ref[:, base + 2 * SB_W:base + 3 * SB_W])
    vb_ref[...] = v.astype(bf16)
    if kv_transposed:
        sk_ref[0] = k.T
        sv_ref[0] = v.T
    else:
        sk_ref[...] = k
        sv_ref[...] = v


def _in_proj(x, g, w_bf, tm, n_seq=None):
    t = x.shape[0]
    d_in = w_bf.shape[1]
    row = lambda n: pl.BlockSpec((tm, n), lambda i: (i, 0))
    if n_seq is None:
        kv_spec, kv_shape = row(SB_W), jax.ShapeDtypeStruct((t, SB_W), f32)
    else:
        nblk = t // n_seq // tm
        kv_spec = pl.BlockSpec((1, SB_W, tm), lambda i: (i // nblk, 0, i % nblk))
        kv_shape = jax.ShapeDtypeStruct((n_seq, SB_W, t // n_seq), f32)
    return pl.pallas_call(
        functools.partial(_in_proj_body, kv_transposed=n_seq is not None),
        grid=(t // tm,),
        in_specs=[row(D_MODEL), _const_spec((1, D_MODEL)), _const_spec((D_MODEL, d_in))],
        out_specs=[row(4 * HG_W), kv_spec, kv_spec, row(SB_W), row(SB_W), row(SB_W)],
        out_shape=[jax.ShapeDtypeStruct((t, 4 * HG_W), f32), kv_shape, kv_shape,
                   jax.ShapeDtypeStruct((t, SB_W), bf16), jax.ShapeDtypeStruct((t, SB_W), bf16),
                   jax.ShapeDtypeStruct((t, SB_W), bf16)],
        compiler_params=_cparams(("arbitrary",)),
        name="in_proj",
    )(x, g, w_bf)


def _hgrn_consts(nb, c):
    r = np.arange(nb)
    same = (r[:, None] // c) == (r[None, :] // c)
    tri = same & (r[None, :] <= r[:, None])
    stack = np.concatenate([tri, same], axis=0).astype(np.float32)
    h = np.arange(HG_W) // HG_DK
    bo = (h[:, None] == h[None, :]).astype(np.float32)
    return jnp.asarray(stack, bf16), jnp.asarray(bo, bf16)


def _hgrn_prep(hin_ref, lbp_ref, stack_ref, scr, nb):
    q_scr, k_scr, v_scr, b_scr, qe_scr, kt_scr, el_scr = scr
    a = lbp_ref[...]
    e = jnp.exp(a - jnp.max(a, axis=0, keepdims=True))
    lb = e[0:1] / jnp.sum(e, axis=0, keepdims=True)
    hq = hin_ref[:, 0:HG_W]
    f = lb + (1.0 - lb) * jax.nn.sigmoid(hin_ref[:, HG_W:2 * HG_W])
    g = jnp.log(f)
    kk = 1.0 - f
    g1 = g.astype(bf16)
    r1 = g - g1.astype(f32)
    g2 = r1.astype(bf16)
    g3 = (r1 - g2.astype(f32)).astype(bf16)
    st = stack_ref[...]
    bb = _dot(st, g1) + _dot(st, g2) + _dot(st, g3)
    b = bb[:nb]
    btot = bb[nb:]
    q_scr[...] = hq
    k_scr[...] = kk
    v_scr[...] = hin_ref[:, 2 * HG_W:3 * HG_W]
    b_scr[...] = b
    qe_scr[...] = hq * jnp.exp(b)
    kt_scr[...] = kk * jnp.exp(btot - b)
    el_scr[...] = jnp.exp(btot)


def _hgrn_chunk(scr, bo, r0, c, st_list):
    q_scr, k_scr, v_scr, b_scr, qe_scr, kt_scr, el_scr = scr
    rows = pl.ds(r0, c)
    b_c = b_scr[rows, :]
    q_c = q_scr[rows, :]
    k_c = k_scr[rows, :]
    v_c = v_scr[rows, :]
    qe_c = qe_scr[rows, :].astype(bf16)
    kt_c = kt_scr[rows, :].astype(bf16)
    el_c = el_scr[pl.ds(r0, 1), :]
    v_cb = v_c.astype(bf16)
    t_idx = lax.broadcasted_iota(jnp.int32, (c, 1), 0)
    slabs = []
    for s in range(c):
        d = jnp.exp(jnp.minimum(b_c - b_c[s:s + 1, :], 0.0))
        slabs.append(jnp.where(t_idx >= s, q_c * d * k_c[s:s + 1, :], 0.0))
    p = jnp.concatenate(slabs, axis=0).astype(bf16)
    pw = _dot(p, bo)
    o = pw[0:c, :] * v_c[0:1, :]
    for s in range(1, c):
        o = o + pw[s * c:(s + 1) * c, :] * v_c[s:s + 1, :]
    ri = lax.broadcasted_iota(jnp.int32, (LANES, LANES), 0) // HG_DK
    ci = lax.broadcasted_iota(jnp.int32, (LANES, LANES), 1) // HG_DK
    same_head = ri == ci
    o_parts, new_states = [], []
    for pr in range(HG_W // LANES):
        sl = slice(pr * LANES, (pr + 1) * LANES)
        st = st_list[pr]
        o_parts.append(_dot_nt(qe_c[:, sl], st.astype(bf16)))
        upd = _dot_tn(v_cb[:, sl], kt_c[:, sl])
        new_states.append(st * el_c[:, sl] + jnp.where(same_head, upd, 0.0))
    return o + jnp.concatenate(o_parts, axis=1), new_states


def _hgrn_finish(o, hin_ref, gn_ref, bo, o_ref):
    ms = _dot((o * o).astype(bf16), bo) * (1.0 / HG_DK)
    gate = hin_ref[:, 3 * HG_W:4 * HG_W]
    o_ref[...] = (o * lax.rsqrt(ms + RMS_EPS) * gn_ref[...] * (gate * jax.nn.sigmoid(gate))).astype(o_ref.dtype)


def _pair_state_out(st):
    t = st.T
    return t[0:HG_DK, 0:HG_DK], t[HG_DK:LANES, HG_DK:LANES]


def _hgrn_prompt_body(hin_ref, lbp_ref, gn_ref, stack_ref, bo_ref, o_ref, sout_ref,
                      st_scr, oacc_scr, *scr, nb, c):
    j = pl.program_id(1)

    @pl.when(j == 0)
    def _():
        st_scr[...] = jnp.zeros_like(st_scr)

    _hgrn_prep(hin_ref, lbp_ref, stack_ref, scr, nb)
    bo = bo_ref[...]

    def step(ci, carry):
        r0 = pl.multiple_of(ci * c, c)
        o, new_states = _hgrn_chunk(scr, bo, r0, c, [st_scr[pr] for pr in range(HG_W // LANES)])
        for pr, st in enumerate(new_states):
            st_scr[pr] = st
        oacc_scr[pl.ds(r0, c), :] = o
        return carry

    lax.fori_loop(0, nb // c, step, 0)
    _hgrn_finish(oacc_scr[...], hin_ref, gn_ref, bo, o_ref)

    @pl.when(j == pl.num_programs(1) - 1)
    def _():
        for pr in range(HG_W // LANES):
            sa, sb = _pair_state_out(st_scr[pr])
            sout_ref[0, 2 * pr] = sa
            sout_ref[0, 2 * pr + 1] = sb


def _hgrn_prompt(hin, lbp, gn, n_seq, seq_len, nb, c):
    stack, bo = _hgrn_consts(nb, c)
    nblk = seq_len // nb
    scr = [pltpu.VMEM((nb, HG_W), f32) for _ in range(7)]
    return pl.pallas_call(
        functools.partial(_hgrn_prompt_body, nb=nb, c=c),
        grid=(n_seq, nblk),
        in_specs=[pl.BlockSpec((nb, 4 * HG_W), lambda b, j: (b * nblk + j, 0)),
                  _const_spec(lbp.shape), _const_spec((1, HG_W)),
                  _const_spec(stack.shape), _const_spec(bo.shape)],
        out_specs=[pl.BlockSpec((nb, HG_W), lambda b, j: (b * nblk + j, 0)),
                   pl.BlockSpec((1, HG_HEADS, HG_DK, HG_DK), lambda b, j: (b, 0, 0, 0))],
        out_shape=[jax.ShapeDtypeStruct((n_seq * seq_len, HG_W), bf16),
                   jax.ShapeDtypeStruct((n_seq, HG_HEADS, HG_DK, HG_DK), f32)],
        scratch_shapes=[pltpu.VMEM((HG_W // LANES, LANES, LANES), f32), pltpu.VMEM((nb, HG_W), f32)] + scr,
        compiler_params=_cparams(("arbitrary", "arbitrary")),
        name="hgrn_prompt",
    )(hin, lbp, gn, stack, bo)


def _hgrn_sample_body(hin_ref, lbp_ref, gn_ref, stack_ref, bo_ref, s0_ref, o_ref, sout_ref,
                      oacc_scr, *scr, nb, c):
    _hgrn_prep(hin_ref, lbp_ref, stack_ref, scr, nb)
    bo = bo_ref[...]
    zero = jnp.zeros((HG_DK, HG_DK), f32)

    def step(ci, carry):
        r0 = pl.multiple_of(ci * c, c)
        states = []
        for pr in range(HG_W // LANES):
            sa = s0_ref[ci, 2 * pr]
            sb = s0_ref[ci, 2 * pr + 1]
            bd = jnp.concatenate([jnp.concatenate([sa, zero], axis=1),
                                  jnp.concatenate([zero, sb], axis=1)], axis=0)
            states.append(bd.T)
        o, new_states = _hgrn_chunk(scr, bo, r0, c, states)
        for pr, st in enumerate(new_states):
            sa, sb = _pair_state_out(st)
            sout_ref[ci, 2 * pr] = sa
            sout_ref[ci, 2 * pr + 1] = sb
        oacc_scr[pl.ds(r0, c), :] = o
        return carry

    lax.fori_loop(0, nb // c, step, 0)
    _hgrn_finish(oacc_scr[...], hin_ref, gn_ref, bo, o_ref)


def _hgrn_sample(hin, lbp, gn, s0, c, seqs_per_step):
    n_seq = s0.shape[0]
    nb = seqs_per_step * c
    stack, bo = _hgrn_consts(nb, c)
    scr = [pltpu.VMEM((nb, HG_W), f32) for _ in range(7)]
    st_spec = pl.BlockSpec((seqs_per_step, HG_HEADS, HG_DK, HG_DK), lambda i: (i, 0, 0, 0))
    return pl.pallas_call(
        functools.partial(_hgrn_sample_body, nb=nb, c=c),
        grid=(n_seq // seqs_per_step,),
        in_specs=[pl.BlockSpec((nb, 4 * HG_W), lambda i: (i, 0)),
                  _const_spec(lbp.shape), _const_spec((1, HG_W)),
                  _const_spec(stack.shape), _const_spec(bo.shape), st_spec],
        out_specs=[pl.BlockSpec((nb, HG_W), lambda i: (i, 0)), st_spec],
        out_shape=[jax.ShapeDtypeStruct((n_seq * c, HG_W), bf16),
                   jax.ShapeDtypeStruct(s0.shape, f32)],
        scratch_shapes=[pltpu.VMEM((nb, HG_W), f32)] + scr,
        compiler_params=_cparams(("arbitrary",)),
        name="hgrn_sample",
    )(hin, lbp, gn, stack, bo, s0)


NEG_BIG = -1e30


def _sb_logs(zp_raw, nbias, r, strict):
    zp = zp_raw + nbias
    c = jnp.minimum(zp, 0.0) - jnp.log2(1.0 + jnp.exp2(-jnp.abs(zp)))
    ls = (c - zp) + r
    if strict is not None:
        c = jnp.where(strict, c, 0.0)
        ls = jnp.where(strict, ls, NEG_BIG)
    return c.astype(bf16), ls, r + jnp.sum(c, axis=-1, keepdims=True)


def _sb_weights(ls, later):
    return jnp.exp2(ls + later).astype(bf16)


def _sb_prompt_body(bias_ref, q_ref, k_ref, v_ref, u_ref, o_ref,
                    qs_scr, zp_scr, c_scr, ls_scr, loc_scr, a_scr, acc_scr, r_scr, *, tq, tk):
    pr = pl.program_id(1)
    i = pl.program_id(2)
    nk = pl.num_programs(2) * (tq // tk)
    lane = lax.broadcasted_iota(jnp.int32, (1, LANES), 1)
    q = q_ref[...]
    zero_q = jnp.zeros_like(q)
    qs_scr[0:tq, :] = jnp.where(lane < SB_DH, q, zero_q)
    qs_scr[tq:2 * tq, :] = jnp.where(lane >= SB_DH, q, zero_q)
    row = lax.broadcasted_iota(jnp.int32, (2 * tq, 1), 0)
    nbias = jnp.where(row < tq, -LOG2E * bias_ref[2 * pr], -LOG2E * bias_ref[2 * pr + 1])
    t_row = lax.broadcasted_iota(jnp.int32, (2 * tq, tk), 0) % tq
    s_col = lax.broadcasted_iota(jnp.int32, (2 * tq, tk), 1)
    top = 2 * i + 1

    def kblock(ref, kb):
        return ref[pl.ds(pl.multiple_of(kb * tk, tk), tk), :]

    def scores(kb):
        return _dot_nt(qs_scr[...], kblock(k_ref, jnp.maximum(kb, 0)))

    c0, ls0, r0 = _sb_logs(scores(top), nbias, jnp.zeros((2 * tq, 1), f32), s_col + tk < t_row)
    c_scr[0] = c0
    ls_scr[0] = ls0
    r_scr[...] = r0
    zp_scr[1] = scores(top - 1)
    a_scr[...] = jnp.zeros_like(a_scr)
    acc_scr[...] = jnp.zeros_like(acc_scr)

    def step(n, slot, strict):
        prev = 1 - slot
        acc_scr[...] += _dot(a_scr[...], kblock(v_ref, jnp.minimum(top - n + 2, nk - 1)))
        loc_scr[...] = _dot(c_scr[prev], u_ref[...])
        zp_scr[prev] = scores(top - n - 1)
        c, ls, r = _sb_logs(zp_scr[slot], nbias, r_scr[...], strict)
        c_scr[slot] = c
        ls_scr[slot] = ls
        r_scr[...] = r
        a_scr[...] = _sb_weights(ls_scr[prev], loc_scr[...])

    step(1, 1, s_col < t_row)

    def pair(j, carry):
        step(2 * j + 2, 0, None)
        step(2 * j + 3, 1, None)
        return carry

    lax.fori_loop(0, i, pair, 0)
    acc = acc_scr[...] + _dot(a_scr[...], kblock(v_ref, 1))
    a_last = _sb_weights(ls_scr[1], _dot(c_scr[1], u_ref[...]))
    acc = acc + _dot(a_last, kblock(v_ref, 0))
    o_ref[...] = jnp.where(lane < SB_DH, acc[:tq], acc[tq:]).astype(o_ref.dtype)


def _strict_upper(n):
    r = np.arange(n)
    return jnp.asarray((r[:, None] > r[None, :]).astype(np.float32), bf16)


def _sb_prompt(qn, kb, vb, bias, n_seq, seq_len, tk):
    tq = 2 * tk
    nq = seq_len // tq
    u = _strict_upper(tk)
    return pl.pallas_call(
        functools.partial(_sb_prompt_body, tq=tq, tk=tk),
        grid=(n_seq, SB_W // LANES, nq),
        in_specs=[pl.BlockSpec(memory_space=pltpu.SMEM),
                  pl.BlockSpec((tq, LANES), lambda b, p, i: (b * nq + i, p)),
                  pl.BlockSpec((seq_len, LANES), lambda b, p, i: (b, p)),
                  pl.BlockSpec((seq_len, LANES), lambda b, p, i: (b, p)),
                  _const_spec(u.shape)],
        out_specs=pl.BlockSpec((tq, LANES), lambda b, p, i: (b * nq + i, p)),
        out_shape=jax.ShapeDtypeStruct((n_seq * seq_len, SB_W), bf16),
        scratch_shapes=[pltpu.VMEM((2 * tq, LANES), bf16),
                        pltpu.VMEM((2, 2 * tq, tk), f32), pltpu.VMEM((2, 2 * tq, tk), bf16),
                        pltpu.VMEM((2, 2 * tq, tk), f32), pltpu.VMEM((2 * tq, tk), f32),
                        pltpu.VMEM((2 * tq, tk), bf16), pltpu.VMEM((2 * tq, LANES), f32),
                        pltpu.VMEM((2 * tq, 1), f32)],
        compiler_params=_cparams(("arbitrary", "arbitrary", "arbitrary")),
        name="sb_prompt",
    )(bias, qn, kb, vb, u)


def _sb_sample_body(pt_ref, bias_ref, q_ref, kn_ref, vn_ref, u_ref, *rest, n_pages, dec):
    k_refs = rest[:n_pages]
    v_refs = rest[n_pages:2 * n_pages]
    o_ref = rest[2 * n_pages]
    rows = SB_HEADS * dec
    lane_head = lax.broadcasted_iota(jnp.int32, (1, SB_W), 1) // SB_DH
    q = q_ref[0].astype(f32)
    qbd = jnp.concatenate([jnp.where(lane_head == h, q, 0.0) for h in range(SB_HEADS)], axis=0).astype(bf16)
    row_head = lax.broadcasted_iota(jnp.int32, (rows, 1), 0) // dec
    nbias = jnp.zeros((rows, 1), f32)
    for h in range(SB_HEADS):
        nbias = jnp.where(row_head == h, -LOG2E * bias_ref[h], nbias)
    pad = jnp.zeros((PAGE - dec, SB_W), f32)
    kn = jnp.concatenate([kn_ref[0].astype(f32), pad], axis=0).astype(bf16)
    vn = jnp.concatenate([vn_ref[0].astype(f32), pad], axis=0).astype(bf16)
    t_row = lax.broadcasted_iota(jnp.int32, (rows, PAGE), 0) % dec
    s_col = lax.broadcasted_iota(jnp.int32, (rows, PAGE), 1)

    def pair(refs, g):
        return jnp.concatenate([refs[2 * g][0], refs[2 * g + 1][0]], axis=1).astype(bf16)

    groups = list(range(n_pages // 2 - 1, -1, -1))
    scores = [_dot_nt(qbd, kn)] + [_dot(qbd, pair(k_refs, g)) for g in groups]
    r = jnp.zeros((rows, 1), f32)
    cs, lss = [], []
    for n, zp in enumerate(scores):
        c, ls, r = _sb_logs(zp, nbias, r, (s_col < t_row) if n == 0 else None)
        cs.append(c)
        lss.append(ls)
    later = [_dot(cs[0], u_ref[0:PAGE, 0:PAGE])] + [_dot(c, u_ref[...]) for c in cs[1:]]
    acc = _dot(_sb_weights(lss[0], later[0]), vn)
    for n, g in enumerate(groups):
        acc = acc + _dot_nt(_sb_weights(lss[n + 1], later[n + 1]), pair(v_refs, g))
    out = jnp.zeros((dec, SB_W), f32)
    for h in range(SB_HEADS):
        out = out + jnp.where(lane_head == h, acc[h * dec:(h + 1) * dec, :], 0.0)
    o_ref[0] = out.astype(o_ref.dtype)


def _sb_sample(qn, kb, vb, cache_k, cache_v, page_table, bias, dec):
    n_seq, n_pages = page_table.shape
    assert n_pages % 2 == 0, "pages are consumed two at a time"
    u = _strict_upper(2 * PAGE)
    pt = page_table.reshape(-1)
    tok = pl.BlockSpec((1, dec, SB_W), lambda n, pt: (n, 0, 0))

    def page_spec(j):
        return pl.BlockSpec((1, SB_W, PAGE), lambda n, pt: (pt[n * n_pages + j], 0, 0))

    grid_spec = pltpu.PrefetchScalarGridSpec(
        num_scalar_prefetch=1,
        grid=(n_seq,),
        in_specs=[pl.BlockSpec(memory_space=pltpu.SMEM), tok, tok, tok,
                  pl.BlockSpec(u.shape, lambda n, pt: (0, 0))]
                 + [page_spec(j) for j in range(n_pages)] * 2,
        out_specs=tok,
    )
    return pl.pallas_call(
        functools.partial(_sb_sample_body, n_pages=n_pages, dec=dec),
        grid_spec=grid_spec,
        out_shape=jax.ShapeDtypeStruct((n_seq, dec, SB_W), bf16),
        compiler_params=_cparams(("arbitrary",)),
        name="sb_sample",
    )(pt, bias, qn.reshape(n_seq, dec, SB_W), kb.reshape(n_seq, dec, SB_W), vb.reshape(n_seq, dec, SB_W), u,
      *([cache_k] * n_pages), *([cache_v] * n_pages))


def _mix_out_body(x_ref, ohg_ref, osb_ref, wo_ref, g1_ref, g2_ref, wq_ref, x1_ref, qc_ref):
    mixed = _dot(ohg_ref[...], wo_ref[0:HG_W, :]) + _dot(osb_ref[...], wo_ref[HG_W:HG_W + SB_W, :])
    x1 = x_ref[...] + _rms(mixed, g1_ref[...])
    x1_ref[...] = x1
    h2 = _rms(x1, g2_ref[...]).astype(bf16)
    qc_ref[...] = (_dot(h2, wq_ref[...]) * CA_SCALE).astype(bf16)


def _mix_out(x, ohg, osb, wo_bf, g1, g2, wq_bf, tm):
    t = x.shape[0]
    row = lambda n: pl.BlockSpec((tm, n), lambda i: (i, 0))
    return pl.pallas_call(
        _mix_out_body,
        grid=(t // tm,),
        in_specs=[row(D_MODEL), row(HG_W), row(SB_W), _const_spec(wo_bf.shape),
                  _const_spec((1, D_MODEL)), _const_spec((1, D_MODEL)), _const_spec(wq_bf.shape)],
        out_specs=[row(D_MODEL), row(D_MODEL)],
        out_shape=[jax.ShapeDtypeStruct((t, D_MODEL), f32), jax.ShapeDtypeStruct((t, D_MODEL), bf16)],
        compiler_params=_cparams(("arbitrary",)),
        name="mix_out",
    )(x, ohg, osb, wo_bf, g1, g2, wq_bf)


def _mem_kv_body(m_ref, g_ref, wk_ref, wv_ref, mk_ref, mv_ref):
    mn = _rms(m_ref[...], g_ref[...]).astype(bf16)
    mk_ref[...] = _dot(mn, wk_ref[...])
    mv_ref[...] = _dot(mn, wv_ref[...])


def _mem_kv(mem, g, wk_bf, wv_bf, tm):
    t = mem.shape[0]
    row = pl.BlockSpec((tm, D_MODEL), lambda i: (i, 0))
    return pl.pallas_call(
        _mem_kv_body,
        grid=(t // tm,),
        in_specs=[row, _const_spec((1, D_MODEL)), _const_spec(wk_bf.shape), _const_spec(wv_bf.shape)],
        out_specs=[row, row],
        out_shape=[jax.ShapeDtypeStruct((t, D_MODEL), f32)] * 2,
        compiler_params=_cparams(("arbitrary",)),
        name="mem_kv",
    )(mem, g, wk_bf, wv_bf)


def _cross_heads(q, mk_head, mv_head):
    outs = []
    for h in range(CA_HEADS):
        s = _dot_nt(q[:, h * CA_DH:(h + 1) * CA_DH], mk_head(h))
        s = s - jnp.max(s, axis=-1, keepdims=True)
        e = jnp.exp(s)
        p = e / jnp.sum(e, axis=-1, keepdims=True)
        outs.append(_dot(p.astype(bf16), mv_head(h)))
    return jnp.concatenate(outs, axis=1)


def _cross_prompt_body(q_ref, mk_ref, mv_ref, o_ref):
    mk = mk_ref[...].astype(bf16)
    mv = mv_ref[...].astype(bf16)
    o_ref[...] = _cross_heads(q_ref[...], lambda h: mk[:, h * CA_DH:(h + 1) * CA_DH],
                              lambda h: mv[:, h * CA_DH:(h + 1) * CA_DH]).astype(o_ref.dtype)


def _cross_prompt(qc, mk, mv, n_seq, seq_len, tm):
    nblk = seq_len // tm
    row = pl.BlockSpec((tm, D_MODEL), lambda b, j: (b * nblk + j, 0))
    mem = pl.BlockSpec((N_MEM, D_MODEL), lambda b, j: (b, 0))
    return pl.pallas_call(
        _cross_prompt_body,
        grid=(n_seq, nblk),
        in_specs=[row, mem, mem],
        out_specs=row,
        out_shape=jax.ShapeDtypeStruct(qc.shape, bf16),
        compiler_params=_cparams(("arbitrary", "arbitrary")),
        name="cross_prompt",
    )(qc, mk, mv)


def _cross_sample_body(q_ref, mk_ref, mv_ref, o_ref, *, group, dec):
    for s in range(group):
        mk = mk_ref[s].astype(bf16)
        mv = mv_ref[s].astype(bf16)
        o_ref[s * dec:(s + 1) * dec, :] = _cross_heads(
            q_ref[s * dec:(s + 1) * dec, :], lambda h: mk[:, h * CA_DH:(h + 1) * CA_DH],
            lambda h: mv[:, h * CA_DH:(h + 1) * CA_DH]).astype(o_ref.dtype)


def _cross_sample(qc, mk, mv, dec, group):
    n_seq = mk.shape[0]
    row = pl.BlockSpec((group * dec, D_MODEL), lambda i: (i, 0))
    mem = pl.BlockSpec((group, N_MEM, D_MODEL), lambda i: (i, 0, 0))
    return pl.pallas_call(
        functools.partial(_cross_sample_body, group=group, dec=dec),
        grid=(n_seq // group,),
        in_specs=[row, mem, mem],
        out_specs=row,
        out_shape=jax.ShapeDtypeStruct(qc.shape, bf16),
        compiler_params=_cparams(("arbitrary",)),
        name="cross_sample",
    )(qc, mk, mv)


def _ca_out_body(ca_ref, x1_ref, wco_ref, g1_ref, g2_ref, x2_ref, h3_ref):
    x2 = x1_ref[...] + _rms(_dot(ca_ref[...], wco_ref[...]), g1_ref[...])
    x2_ref[...] = x2
    h3_ref[...] = _rms(x2, g2_ref[...]).astype(bf16)


def _ca_out(ca, x1, wco_bf, g1, g2, tm):
    t = x1.shape[0]
    row = pl.BlockSpec((tm, D_MODEL), lambda i: (i, 0))
    return pl.pallas_call(
        _ca_out_body,
        grid=(t // tm,),
        in_specs=[row, row, _const_spec(wco_bf.shape), _const_spec((1, D_MODEL)), _const_spec((1, D_MODEL))],
        out_specs=[row, row],
        out_shape=[jax.ShapeDtypeStruct((t, D_MODEL), f32), jax.ShapeDtypeStruct((t, D_MODEL), bf16)],
        compiler_params=_cparams(("arbitrary",)),
        name="ca_out",
    )(ca, x1, wco_bf, g1, g2)


FFN_CHUNK = 256


def _ffn_body(*refs, tm, dec, carry_tail):
    if carry_tail:
        h3_ref, x2_ref, wup_ref, cw_ref, cb_ref, wdn_ref, g_ref, y_ref, tail_ref, tail_scr, act_scr = refs
        halo_ref = None

        @pl.when(pl.program_id(1) == 0)
        def _():
            tail_scr[...] = jnp.zeros_like(tail_scr)
    else:
        h3_ref, x2_ref, halo_ref, wup_ref, cw_ref, cb_ref, wdn_ref, g_ref, y_ref, u_ref, act_scr = refs
        pos = lax.broadcasted_iota(jnp.int32, (tm, 1), 0) % dec
    h3 = h3_ref[...]

    def conv_chunk(col):
        cs = slice(col, col + FFN_CHUNK)
        u = _dot(h3, wup_ref[:, cs])
        if carry_tail:
            ext = jnp.concatenate([tail_scr[:, cs], u], axis=0)
            u1 = ext[7:7 + tm]
            u2 = ext[6:6 + tm]
            tail_scr[:, cs] = u[tm - 8:tm]
            tail_ref[:, cs] = u[tm - 8:tm]
        else:
            ext = jnp.concatenate([jnp.zeros((8, FFN_CHUNK), f32), u], axis=0)
            hal = halo_ref[:, cs]
            hext = jnp.concatenate([hal, jnp.zeros((8, FFN_CHUNK), f32)], axis=0)
            u1 = jnp.where(pos == 0, hext[1:1 + tm], ext[7:7 + tm])
            u2 = jnp.where(pos < 2, hal, ext[6:6 + tm])
            u_ref[:, cs] = u
        return cb_ref[:, cs] + cw_ref[2:3, cs] * u + cw_ref[1:2, cs] * u1 + cw_ref[0:1, cs] * u2

    for cj in range(D_FF // FFN_CHUNK):
        gate = conv_chunk(cj * FFN_CHUNK)
        val = conv_chunk(D_FF + cj * FFN_CHUNK)
        act_scr[:, cj * FFN_CHUNK:(cj + 1) * FFN_CHUNK] = (jax.nn.gelu(gate, approximate=True) * val).astype(bf16)
    y = _dot(act_scr[...], wdn_ref[...])
    y_ref[...] = x2_ref[...] + _rms(y, g_ref[...])


def _ffn_prompt(h3, x2, wup_bf, cw, cb, wdn_bf, g, n_seq, seq_len, tm):
    nblk = seq_len // tm
    row = pl.BlockSpec((tm, D_MODEL), lambda b, j: (b * nblk + j, 0))
    return pl.pallas_call(
        functools.partial(_ffn_body, tm=tm, dec=None, carry_tail=True),
        grid=(n_seq, nblk),
        in_specs=[row, row, _const_spec(wup_bf.shape), _const_spec(cw.shape), _const_spec(cb.shape),
                  _const_spec(wdn_bf.shape), _const_spec((1, D_MODEL))],
        out_specs=[row, pl.BlockSpec((8, 2 * D_FF), lambda b, j: (b, 0))],
        out_shape=[jax.ShapeDtypeStruct((n_seq * seq_len, D_MODEL), f32),
                   jax.ShapeDtypeStruct((n_seq * 8, 2 * D_FF), f32)],
        scratch_shapes=[pltpu.VMEM((8, 2 * D_FF), f32), pltpu.VMEM((tm, D_FF), bf16)],
        compiler_params=_cparams(("arbitrary", "arbitrary")),
        name="ffn_prompt",
    )(h3, x2, wup_bf, cw, cb, wdn_bf, g)


def _ffn_sample(h3, x2, halo, wup_bf, cw, cb, wdn_bf, g, dec, tm):
    t = h3.shape[0]
    row = pl.BlockSpec((tm, D_MODEL), lambda i: (i, 0))
    wide = pl.BlockSpec((tm, 2 * D_FF), lambda i: (i, 0))
    return pl.pallas_call(
        functools.partial(_ffn_body, tm=tm, dec=dec, carry_tail=False),
        grid=(t // tm,),
        in_specs=[row, row, wide, _const_spec(wup_bf.shape), _const_spec(cw.shape), _const_spec(cb.shape),
                  _const_spec(wdn_bf.shape), _const_spec((1, D_MODEL))],
        out_specs=[row, wide],
        out_shape=[jax.ShapeDtypeStruct((t, D_MODEL), f32), jax.ShapeDtypeStruct((t, 2 * D_FF), f32)],
        scratch_shapes=[pltpu.VMEM((tm, D_FF), bf16)],
        compiler_params=_cparams(("arbitrary",)),
        name="ffn_sample",
    )(h3, x2, halo, wup_bf, cw, cb, wdn_bf, g)


def kernel(x_prompt, x_sample, cache_sb_k, cache_sb_v, state_hgrn, state_ffn_conv, cache_mem_k, cache_mem_v,
           page_table, mem_prompt, w_in, hg_norm, hg_lb, sb_bias, w_o, g_mix_pre, g_mix_post, g_ca_pre, g_ca_post,
           g_mem, w_cq, w_ck, w_cv, w_co, g_ffn_pre, g_ffn_post, w_up, conv_w, conv_b, w_down):
    n_p, seq_len, _ = x_prompt.shape
    n_d, dec, _ = x_sample.shape
    depth = w_in.shape[0]
    assert depth == 1, "single-layer step"
    assert dec >= 2 and dec % 8 == 0, "the conv tail is taken from the new rows"
    l = 0
    row = lambda a: a[l].reshape(1, -1)
    w_in_bf, w_o_bf, w_cq_bf, w_co_bf = (w[l].astype(bf16) for w in (w_in, w_o, w_cq, w_co))
    w_ck_bf, w_cv_bf, w_up_bf, w_dn_bf = (w[l].astype(bf16) for w in (w_ck, w_cv, w_up, w_down))
    lbp = hg_lb[l:l + 2]
    gn = row(hg_norm)
    bias = sb_bias[l]
    cb = row(conv_b)
    cw = conv_w[l]

    xp = x_prompt.reshape(n_p * seq_len, D_MODEL)
    xs = x_sample.reshape(n_d * dec, D_MODEL)

    hin, sk_t, sv_t, qn, kb, vb = _in_proj(xp, row(g_mix_pre), w_in_bf, 512, n_seq=n_p)
    ohg, s_p = _hgrn_prompt(hin, lbp, gn, n_p, seq_len, 256, 16)
    osb = _sb_prompt(qn, kb, vb, bias, n_p, seq_len, 256)
    x1, qc = _mix_out(xp, ohg, osb, w_o_bf, row(g_mix_post), row(g_ca_pre), w_cq_bf, 512)
    mk_p, mv_p = _mem_kv(mem_prompt.reshape(n_p * N_MEM, D_MODEL), row(g_mem), w_ck_bf, w_cv_bf, 256)
    ca = _cross_prompt(qc, mk_p, mv_p, n_p, seq_len, 512)
    x2, h3 = _ca_out(ca, x1, w_co_bf, row(g_ca_post), row(g_ffn_pre), 512)
    yp, tail_p = _ffn_prompt(h3, x2, w_up_bf, cw, cb, w_dn_bf, row(g_ffn_post), n_p, seq_len, 256)

    ts = n_d * dec
    hin_s, sk_s, sv_s, qn_s, kb_s, vb_s = _in_proj(xs, row(g_mix_pre), w_in_bf, 512)
    ohg_s, s_s = _hgrn_sample(hin_s, lbp, gn, state_hgrn[l], dec, 16)
    n_pool = cache_sb_k.shape[1]
    pages_t = lambda c: jnp.transpose(c[l], (0, 2, 3, 1)).reshape(n_pool, SB_W, PAGE)
    osb_s = _sb_sample(qn_s, kb_s, vb_s, pages_t(cache_sb_k), pages_t(cache_sb_v), page_table, bias,
                       dec).reshape(ts, SB_W)
    x1_s, qc_s = _mix_out(xs, ohg_s, osb_s, w_o_bf, row(g_mix_post), row(g_ca_pre), w_cq_bf, 512)
    ca_s = _cross_sample(qc_s, cache_mem_k[l].reshape(n_d, N_MEM, D_MODEL),
                         cache_mem_v[l].reshape(n_d, N_MEM, D_MODEL), dec, 4)
    x2_s, h3_s = _ca_out(ca_s, x1_s, w_co_bf, row(g_ca_post), row(g_ffn_pre), 512)
    halo = jnp.pad(state_ffn_conv[l], ((0, 0), (0, dec - 2), (0, 0))).reshape(ts, 2 * D_FF)
    ys, u_s = _ffn_sample(h3_s, x2_s, halo, w_up_bf, cw, cb, w_dn_bf, row(g_ffn_post), dec, 128)

    kv_out = lambda a: jnp.transpose(a.reshape(n_p, SB_HEADS, SB_DH, seq_len), (0, 3, 1, 2))[None]
    return (yp.reshape(n_p, seq_len, D_MODEL), ys.reshape(n_d, dec, D_MODEL),
            kv_out(sk_t), kv_out(sv_t),
            s_p[None],
            tail_p.reshape(n_p, 8, 2 * D_FF)[None, :, 6:8],
            mk_p.reshape(1, n_p, N_MEM, CA_HEADS, CA_DH), mv_p.reshape(1, n_p, N_MEM, CA_HEADS, CA_DH),
            sk_s.reshape(1, n_d, dec, SB_HEADS, SB_DH), sv_s.reshape(1, n_d, dec, SB_HEADS, SB_DH),
            s_s[None],
            u_s.reshape(n_d, dec, 2 * D_FF)[None, :, dec - 2:dec])
```

```python
import functools

import numpy as np
import jax
import jax.numpy as jnp
from jax import lax
from jax.experimental import pallas as pl
from jax.experimental.pallas import tpu as pltpu

f32 = jnp.float32
bf16 = jnp.bfloat16

D_MODEL = 1024
HG_HEADS = 8
HG_DK = 64
HG_W = HG_HEADS * HG_DK
SB_HEADS = 8
SB_DH = 64
SB_W = SB_HEADS * SB_DH
SB_SCALE = SB_DH ** -0.5
LOG2E = 1.4426950408889634
N_MEM = 256
CA_HEADS = 4
CA_DH = D_MODEL // CA_HEADS
CA_SCALE = CA_DH ** -0.5
D_FF = 2816
RMS_EPS = 1e-6
PAGE = 128
LANES = 128
SUB = 8
VMEM_LIMIT = 56 * 1024 * 1024


def _cparams(sem, flags=None):
    return pltpu.CompilerParams(dimension_semantics=sem, vmem_limit_bytes=VMEM_LIMIT, flags=flags)


def _const_spec(shape):
    nd = len(shape)
    return pl.BlockSpec(shape, lambda *_: (0,) * nd, pipeline_mode=pl.Buffered(1))


def _rms(x, g):
    return x * lax.rsqrt(jnp.mean(x * x, axis=-1, keepdims=True) + RMS_EPS) * g


def _dot(a, b):
    return jnp.dot(a, b, preferred_element_type=f32)


def _dot_nt(a, b):
    return lax.dot_general(a, b, (((1,), (1,)), ((), ())), preferred_element_type=f32)


def _dot_tn(a, b):
    return lax.dot_general(a, b, (((0,), (0,)), ((), ())), preferred_element_type=f32)


def _in_proj_body(x_ref, g_ref, w_ref, hin_ref, sk_ref, sv_ref, qn_ref, kb_ref, vb_ref, *, kv_transposed):
    xn = _rms(x_ref[...], g_ref[...]).astype(bf16)
    for j in range(4):
        hin_ref[:, j * HG_W:(j + 1) * HG_W] = _dot(xn, w_ref[:, j * HG_W:(j + 1) * HG_W])
    base = 4 * HG_W
    q = _dot(xn, w_ref[:, base:base + SB_W])
    qn_ref[...] = (q * (-SB_SCALE * LOG2E)).astype(bf16)
    k = _dot(xn, w_ref[:, base + SB_W:base + 2 * SB_W])
    kb_ref[...] = k.astype(bf16)
    v = _dot(xn, w_ref[:, base + 2 * SB_W:base + 3 * SB_W])
    vb_ref[...] = v.astype(bf16)
    if kv_transposed:
        sk_ref[0] = k.T
        sv_ref[0] = v.T
    else:
        sk_ref[...] = k
        sv_ref[...] = v


def _in_proj(x, g, w_bf, tm, n_seq=None):
    t = x.shape[0]
    d_in = w_bf.shape[1]
    row = lambda n: pl.BlockSpec((tm, n), lambda i: (i, 0))
    if n_seq is None:
        kv_spec, kv_shape = row(SB_W), jax.ShapeDtypeStruct((t, SB_W), f32)
    else:
        nblk = t // n_seq // tm
        kv_spec = pl.BlockSpec((1, SB_W, tm), lambda i: (i // nblk, 0, i % nblk))
        kv_shape = jax.ShapeDtypeStruct((n_seq, SB_W, t // n_seq), f32)
    return pl.pallas_call(
        functools.partial(_in_proj_body, kv_transposed=n_seq is not None),
        grid=(t // tm,),
        in_specs=[row(D_MODEL), _const_spec((1, D_MODEL)), _const_spec((D_MODEL, d_in))],
        out_specs=[row(4 * HG_W), kv_spec, kv_spec, row(SB_W), row(SB_W), row(SB_W)],
        out_shape=[jax.ShapeDtypeStruct((t, 4 * HG_W), f32), kv_shape, kv_shape,
                   jax.ShapeDtypeStruct((t, SB_W), bf16), jax.ShapeDtypeStruct((t, SB_W), bf16),
                   jax.ShapeDtypeStruct((t, SB_W), bf16)],
        compiler_params=_cparams(("arbitrary",)),
        name="in_proj",
    )(x, g, w_bf)


def _hgrn_consts(nb, c):
    r = np.arange(nb)
    same = (r[:, None] // c) == (r[None, :] // c)
    tri = same & (r[None, :] <= r[:, None])
    stack = np.concatenate([tri, same], axis=0).astype(np.float32)
    h = np.arange(HG_W) // HG_DK
    bo = (h[:, None] == h[None, :]).astype(np.float32)
    return jnp.asarray(stack, bf16), jnp.asarray(bo, bf16)


def _hgrn_prep(hin_ref, lbp_ref, stack_ref, scr, nb):
    q_scr, e_scr, v_scr, b_scr, qe_scr, kt_scr, el_scr = scr
    a = lbp_ref[...]
    e = jnp.exp(a - jnp.max(a, axis=0, keepdims=True))
    lb = e[0:1] / jnp.sum(e, axis=0, keepdims=True)
    hq = hin_ref[:, 0:HG_W]
    f = lb + (1.0 - lb) * jax.nn.sigmoid(hin_ref[:, HG_W:2 * HG_W])
    g = jnp.log2(f)
    kk = 1.0 - f
    g1 = g.astype(bf16)
    r1 = g - g1.astype(f32)
    g2 = r1.astype(bf16)
    g3 = (r1 - g2.astype(f32)).astype(bf16)
    st = stack_ref[...]
    bb = _dot(st, g1) + _dot(st, g2) + _dot(st, g3)
    b = bb[:nb]
    btot = bb[nb:]
    q_scr[...] = hq
    e_scr[...] = b - jnp.log2(kk)
    v_scr[...] = hin_ref[:, 2 * HG_W:3 * HG_W]
    b_scr[...] = b
    qe_scr[...] = hq * jnp.exp2(b)
    kt_scr[...] = kk * jnp.exp2(btot - b)
    el_scr[...] = jnp.exp2(btot)


def _hgrn_chunk(scr, bo, r0, c, st_list):
    q_scr, e_scr, v_scr, b_scr, qe_scr, kt_scr, el_scr = scr
    rows = pl.ds(r0, c)
    b_c = b_scr[rows, :]
    q_c = q_scr[rows, :]
    e_c = e_scr[rows, :]
    v_c = v_scr[rows, :]
    qe_c = qe_scr[rows, :].astype(bf16)
    kt_c = kt_scr[rows, :].astype(bf16)
    el_c = el_scr[pl.ds(r0, 1), :]
    v_cb = v_c.astype(bf16)
    n_sub = c // SUB
    t_loc = lax.broadcasted_iota(jnp.int32, (SUB, 1), 0)
    pieces, index = [], {}
    for s in range(c):
        for tg in range(s // SUB, n_sub):
            sl = slice(tg * SUB, (tg + 1) * SUB)
            pc = q_c[sl, :] * jnp.exp2(b_c[sl, :] - e_c[s:s + 1, :])
            if tg == s // SUB:
                pc = jnp.where(t_loc >= s % SUB, pc, 0.0)
            index[(s, tg)] = len(pieces)
            pieces.append(pc)
    p = jnp.concatenate(pieces, axis=0).astype(bf16)
    pw = _dot(p, bo)
    groups = []
    for tg in range(n_sub):
        acc = None
        for s in range((tg + 1) * SUB):
            k = index[(s, tg)]
            term = pw[k * SUB:(k + 1) * SUB, :] * v_c[s:s + 1, :]
            acc = term if acc is None else acc + term
        groups.append(acc)
    o = jnp.concatenate(groups, axis=0)
    ri = lax.broadcasted_iota(jnp.int32, (LANES, LANES), 0) // HG_DK
    ci = lax.broadcasted_iota(jnp.int32, (LANES, LANES), 1) // HG_DK
    same_head = ri == ci
    o_parts, new_states = [], []
    for pr in range(HG_W // LANES):
        sl = slice(pr * LANES, (pr + 1) * LANES)
        st = st_list[pr]
        o_parts.append(_dot_nt(qe_c[:, sl], st.astype(bf16)))
        upd = _dot_tn(v_cb[:, sl], kt_c[:, sl])
        new_states.append(st * el_c[:, sl] + jnp.where(same_head, upd, 0.0))
    return o + jnp.concatenate(o_parts, axis=1), new_states


def _hgrn_finish(o, hin_ref, gn_ref, bo, o_ref):
    ms = _dot((o * o).astype(bf16), bo) * (1.0 / HG_DK)
    gate = hin_ref[:, 3 * HG_W:4 * HG_W]
    o_ref[...] = (o * lax.rsqrt(ms + RMS_EPS) * gn_ref[...] * (gate * jax.nn.sigmoid(gate))).astype(o_ref.dtype)


def _pair_state_out(st):
    t = st.T
    return t[0:HG_DK, 0:HG_DK], t[HG_DK:LANES, HG_DK:LANES]


def _hgrn_prompt_body(hin_ref, lbp_ref, gn_ref, stack_ref, bo_ref, o_ref, sout_ref,
                      st_scr, oacc_scr, *scr, nb, c):
    j = pl.program_id(1)

    @pl.when(j == 0)
    def _():
        st_scr[...] = jnp.zeros_like(st_scr)

    _hgrn_prep(hin_ref, lbp_ref, stack_ref, scr, nb)
    bo = bo_ref[...]

    def step(ci, carry):
        r0 = pl.multiple_of(ci * c, c)
        o, new_states = _hgrn_chunk(scr, bo, r0, c, [st_scr[pr] for pr in range(HG_W // LANES)])
        for pr, st in enumerate(new_states):
            st_scr[pr] = st
        oacc_scr[pl.ds(r0, c), :] = o
        return carry

    lax.fori_loop(0, nb // c, step, 0, unroll=2)
    _hgrn_finish(oacc_scr[...], hin_ref, gn_ref, bo, o_ref)

    @pl.when(j == pl.num_programs(1) - 1)
    def _():
        for pr in range(HG_W // LANES):
            sa, sb = _pair_state_out(st_scr[pr])
            sout_ref[0, 2 * pr] = sa
            sout_ref[0, 2 * pr + 1] = sb


def _hgrn_prompt(hin, lbp, gn, n_seq, seq_len, nb, c):
    stack, bo = _hgrn_consts(nb, c)
    nblk = seq_len // nb
    scr = [pltpu.VMEM((nb, HG_W), f32) for _ in range(7)]
    return pl.pallas_call(
        functools.partial(_hgrn_prompt_body, nb=nb, c=c),
        grid=(n_seq, nblk),
        in_specs=[pl.BlockSpec((nb, 4 * HG_W), lambda b, j: (b * nblk + j, 0)),
                  _const_spec(lbp.shape), _const_spec((1, HG_W)),
                  _const_spec(stack.shape), _const_spec(bo.shape)],
        out_specs=[pl.BlockSpec((nb, HG_W), lambda b, j: (b * nblk + j, 0)),
                   pl.BlockSpec((1, HG_HEADS, HG_DK, HG_DK), lambda b, j: (b, 0, 0, 0))],
        out_shape=[jax.ShapeDtypeStruct((n_seq * seq_len, HG_W), bf16),
                   jax.ShapeDtypeStruct((n_seq, HG_HEADS, HG_DK, HG_DK), f32)],
        scratch_shapes=[pltpu.VMEM((HG_W // LANES, LANES, LANES), f32), pltpu.VMEM((nb, HG_W), f32)] + scr,
        compiler_params=_cparams(("arbitrary", "arbitrary")),
        name="hgrn_prompt",
    )(hin, lbp, gn, stack, bo)


def _hgrn_sample_body(hin_ref, lbp_ref, gn_ref, stack_ref, bo_ref, s0_ref, o_ref, sout_ref,
                      oacc_scr, *scr, nb, c):
    _hgrn_prep(hin_ref, lbp_ref, stack_ref, scr, nb)
    bo = bo_ref[...]
    zero = jnp.zeros((HG_DK, HG_DK), f32)

    def step(ci, carry):
        r0 = pl.multiple_of(ci * c, c)
        states = []
        for pr in range(HG_W // LANES):
            sa = s0_ref[ci, 2 * pr]
            sb = s0_ref[ci, 2 * pr + 1]
            bd = jnp.concatenate([jnp.concatenate([sa, zero], axis=1),
                                  jnp.concatenate([zero, sb], axis=1)], axis=0)
            states.append(bd.T)
        o, new_states = _hgrn_chunk(scr, bo, r0, c, states)
        for pr, st in enumerate(new_states):
            sa, sb = _pair_state_out(st)
            sout_ref[ci, 2 * pr] = sa
            sout_ref[ci, 2 * pr + 1] = sb
        oacc_scr[pl.ds(r0, c), :] = o
        return carry

    lax.fori_loop(0, nb // c, step, 0, unroll=2)
    _hgrn_finish(oacc_scr[...], hin_ref, gn_ref, bo, o_ref)


def _hgrn_sample(hin, lbp, gn, s0, c, seqs_per_step):
    n_seq = s0.shape[0]
    nb = seqs_per_step * c
    stack, bo = _hgrn_consts(nb, c)
    scr = [pltpu.VMEM((nb, HG_W), f32) for _ in range(7)]
    st_spec = pl.BlockSpec((seqs_per_step, HG_HEADS, HG_DK, HG_DK), lambda i: (i, 0, 0, 0))
    return pl.pallas_call(
        functools.partial(_hgrn_sample_body, nb=nb, c=c),
        grid=(n_seq // seqs_per_step,),
        in_specs=[pl.BlockSpec((nb, 4 * HG_W), lambda i: (i, 0)),
                  _const_spec(lbp.shape), _const_spec((1, HG_W)),
                  _const_spec(stack.shape), _const_spec(bo.shape), st_spec],
        out_specs=[pl.BlockSpec((nb, HG_W), lambda i: (i, 0)), st_spec],
        out_shape=[jax.ShapeDtypeStruct((n_seq * c, HG_W), bf16),
                   jax.ShapeDtypeStruct(s0.shape, f32)],
        scratch_shapes=[pltpu.VMEM((nb, HG_W), f32)] + scr,
        compiler_params=_cparams(("arbitrary",)),
        name="hgrn_sample",
    )(hin, lbp, gn, stack, bo, s0)


NEG_BIG = -1e30


def _sb_logs(zp_raw, nbias, r, strict):
    zp = zp_raw + nbias
    c = jnp.minimum(zp, 0.0) - jnp.log2(1.0 + jnp.exp2(-jnp.abs(zp)))
    ls = (c - zp) + r
    if strict is not None:
        c = jnp.where(strict, c, 0.0)
        ls = jnp.where(strict, ls, NEG_BIG)
    return c.astype(bf16), ls, r + jnp.sum(c, axis=-1, keepdims=True)


def _sb_weights(ls, later):
    return jnp.exp2(ls + later).astype(bf16)


def _sb_prompt_body(bias_ref, q_ref, k_ref, v_ref, u_ref, o_ref,
                    qs_scr, zp_scr, c_scr, ls_scr, loc_scr, a_scr, acc_scr, r_scr, *, tq, tk):
    pr = pl.program_id(1)
    i = pl.program_id(2)
    nk = pl.num_programs(2) * (tq // tk)
    lane = lax.broadcasted_iota(jnp.int32, (1, LANES), 1)
    q = q_ref[...]
    zero_q = jnp.zeros_like(q)
    qs_scr[0:tq, :] = jnp.where(lane < SB_DH, q, zero_q)
    qs_scr[tq:2 * tq, :] = jnp.where(lane >= SB_DH, q, zero_q)
    row = lax.broadcasted_iota(jnp.int32, (2 * tq, 1), 0)
    nbias = jnp.where(row < tq, -LOG2E * bias_ref[2 * pr], -LOG2E * bias_ref[2 * pr + 1])
    t_row = lax.broadcasted_iota(jnp.int32, (2 * tq, tk), 0) % tq
    s_col = lax.broadcasted_iota(jnp.int32, (2 * tq, tk), 1)
    top = 2 * i + 1

    def kblock(ref, kb):
        return ref[pl.ds(pl.multiple_of(kb * tk, tk), tk), :]

    def scores(kb):
        return _dot_nt(qs_scr[...], kblock(k_ref, jnp.maximum(kb, 0)))

    c0, ls0, r0 = _sb_logs(scores(top), nbias, jnp.zeros((2 * tq, 1), f32), s_col + tk < t_row)
    c_scr[0] = c0
    ls_scr[0] = ls0
    r_scr[...] = r0
    zp_scr[1] = scores(top - 1)
    a_scr[...] = jnp.zeros_like(a_scr)
    acc_scr[...] = jnp.zeros_like(acc_scr)

    def step(n, slot, strict):
        prev = 1 - slot
        acc_scr[...] += _dot(a_scr[...], kblock(v_ref, jnp.minimum(top - n + 2, nk - 1)))
        loc_scr[...] = _dot(c_scr[prev], u_ref[...])
        zp_scr[prev] = scores(top - n - 1)
        c, ls, r = _sb_logs(zp_scr[slot], nbias, r_scr[...], strict)
        c_scr[slot] = c
        ls_scr[slot] = ls
        r_scr[...] = r
        a_scr[...] = _sb_weights(ls_scr[prev], loc_scr[...])

    step(1, 1, s_col < t_row)

    def pair(j, carry):
        step(2 * j + 2, 0, None)
        step(2 * j + 3, 1, None)
        return carry

    lax.fori_loop(0, i, pair, 0)
    acc = acc_scr[...] + _dot(a_scr[...], kblock(v_ref, 1))
    a_last = _sb_weights(ls_scr[1], _dot(c_scr[1], u_ref[...]))
    acc = acc + _dot(a_last, kblock(v_ref, 0))
    o_ref[...] = jnp.where(lane < SB_DH, acc[:tq], acc[tq:]).astype(o_ref.dtype)


def _strict_upper(n):
    r = np.arange(n)
    return jnp.asarray((r[:, None] > r[None, :]).astype(np.float32), bf16)


def _sb_prompt(qn, kb, vb, bias, n_seq, seq_len, tk):
    tq = 2 * tk
    nq = seq_len // tq
    u = _strict_upper(tk)
    return pl.pallas_call(
        functools.partial(_sb_prompt_body, tq=tq, tk=tk),
        grid=(n_seq, SB_W // LANES, nq),
        in_specs=[pl.BlockSpec(memory_space=pltpu.SMEM),
                  pl.BlockSpec((tq, LANES), lambda b, p, i: (b * nq + i, p)),
                  pl.BlockSpec((seq_len, LANES), lambda b, p, i: (b, p)),
                  pl.BlockSpec((seq_len, LANES), lambda b, p, i: (b, p)),
                  _const_spec(u.shape)],
        out_specs=pl.BlockSpec((tq, LANES), lambda b, p, i: (b * nq + i, p)),
        out_shape=jax.ShapeDtypeStruct((n_seq * seq_len, SB_W), bf16),
        scratch_shapes=[pltpu.VMEM((2 * tq, LANES), bf16),
                        pltpu.VMEM((2, 2 * tq, tk), f32), pltpu.VMEM((2, 2 * tq, tk), bf16),
                        pltpu.VMEM((2, 2 * tq, tk), f32), pltpu.VMEM((2 * tq, tk), f32),
                        pltpu.VMEM((2 * tq, tk), bf16), pltpu.VMEM((2 * tq, LANES), f32),
                        pltpu.VMEM((2 * tq, 1), f32)],
        compiler_params=_cparams(("arbitrary", "arbitrary", "arbitrary")),
        name="sb_prompt",
    )(bias, qn, kb, vb, u)


def _sb_sample_body(pt_ref, bias_ref, q_ref, kn_ref, vn_ref, u_ref, *rest, n_pages, dec):
    k_refs = rest[:n_pages]
    v_refs = rest[n_pages:2 * n_pages]
    o_ref = rest[2 * n_pages]
    rows = SB_HEADS * dec
    lane_head = lax.broadcasted_iota(jnp.int32, (1, SB_W), 1) // SB_DH
    q = q_ref[0].astype(f32)
    qbd = jnp.concatenate([jnp.where(lane_head == h, q, 0.0) for h in range(SB_HEADS)], axis=0).astype(bf16)
    row_head = lax.broadcasted_iota(jnp.int32, (rows, 1), 0) // dec
    nbias = jnp.zeros((rows, 1), f32)
    for h in range(SB_HEADS):
        nbias = jnp.where(row_head == h, -LOG2E * bias_ref[h], nbias)
    pad = jnp.zeros((PAGE - dec, SB_W), f32)
    kn = jnp.concatenate([kn_ref[0].astype(f32), pad], axis=0).astype(bf16)
    vn = jnp.concatenate([vn_ref[0].astype(f32), pad], axis=0).astype(bf16)
    t_row = lax.broadcasted_iota(jnp.int32, (rows, PAGE), 0) % dec
    s_col = lax.broadcasted_iota(jnp.int32, (rows, PAGE), 1)

    def pair(refs, g):
        return jnp.concatenate([refs[2 * g][0], refs[2 * g + 1][0]], axis=1).astype(bf16)

    groups = list(range(n_pages // 2 - 1, -1, -1))
    scores = [_dot_nt(qbd, kn)] + [_dot(qbd, pair(k_refs, g)) for g in groups]
    r = jnp.zeros((rows, 1), f32)
    cs, lss = [], []
    for n, zp in enumerate(scores):
        c, ls, r = _sb_logs(zp, nbias, r, (s_col < t_row) if n == 0 else None)
        cs.append(c)
        lss.append(ls)
    later = [_dot(cs[0], u_ref[0:PAGE, 0:PAGE])] + [_dot(c, u_ref[...]) for c in cs[1:]]
    acc = _dot(_sb_weights(lss[0], later[0]), vn)
    for n, g in enumerate(groups):
        acc = acc + _dot_nt(_sb_weights(lss[n + 1], later[n + 1]), pair(v_refs, g))
    out = jnp.zeros((dec, SB_W), f32)
    for h in range(SB_HEADS):
        out = out + jnp.where(lane_head == h, acc[h * dec:(h + 1) * dec, :], 0.0)
    o_ref[0] = out.astype(o_ref.dtype)


def _sb_sample(qn, kb, vb, cache_k, cache_v, page_table, bias, dec):
    n_seq, n_pages = page_table.shape
    assert n_pages % 2 == 0, "pages are consumed two at a time"
    u = _strict_upper(2 * PAGE)
    pt = page_table.reshape(-1)
    tok = pl.BlockSpec((1, dec, SB_W), lambda n, pt: (n, 0, 0))

    def page_spec(j):
        return pl.BlockSpec((1, SB_W, PAGE), lambda n, pt: (pt[n * n_pages + j], 0, 0))

    grid_spec = pltpu.PrefetchScalarGridSpec(
        num_scalar_prefetch=1,
        grid=(n_seq,),
        in_specs=[pl.BlockSpec(memory_space=pltpu.SMEM), tok, tok, tok,
                  pl.BlockSpec(u.shape, lambda n, pt: (0, 0))]
                 + [page_spec(j) for j in range(n_pages)] * 2,
        out_specs=tok,
    )
    return pl.pallas_call(
        functools.partial(_sb_sample_body, n_pages=n_pages, dec=dec),
        grid_spec=grid_spec,
        out_shape=jax.ShapeDtypeStruct((n_seq, dec, SB_W), bf16),
        compiler_params=_cparams(("arbitrary",)),
        name="sb_sample",
    )(pt, bias, qn.reshape(n_seq, dec, SB_W), kb.reshape(n_seq, dec, SB_W), vb.reshape(n_seq, dec, SB_W), u,
      *([cache_k] * n_pages), *([cache_v] * n_pages))


def _mix_out_body(x_ref, ohg_ref, osb_ref, wo_ref, g1_ref, g2_ref, wq_ref, x1_ref, qc_ref):
    mixed = _dot(ohg_ref[...], wo_ref[0:HG_W, :]) + _dot(osb_ref[...], wo_ref[HG_W:HG_W + SB_W, :])
    x1 = x_ref[...] + _rms(mixed, g1_ref[...])
    x1_ref[...] = x1
    h2 = _rms(x1, g2_ref[...]).astype(bf16)
    qc_ref[...] = (_dot(h2, wq_ref[...]) * CA_SCALE).astype(bf16)


def _mix_out(x, ohg, osb, wo_bf, g1, g2, wq_bf, tm):
    t = x.shape[0]
    row = lambda n: pl.BlockSpec((tm, n), lambda i: (i, 0))
    return pl.pallas_call(
        _mix_out_body,
        grid=(t // tm,),
        in_specs=[row(D_MODEL), row(HG_W), row(SB_W), _const_spec(wo_bf.shape),
                  _const_spec((1, D_MODEL)), _const_spec((1, D_MODEL)), _const_spec(wq_bf.shape)],
        out_specs=[row(D_MODEL), row(D_MODEL)],
        out_shape=[jax.ShapeDtypeStruct((t, D_MODEL), f32), jax.ShapeDtypeStruct((t, D_MODEL), bf16)],
        compiler_params=_cparams(("arbitrary",)),
        name="mix_out",
    )(x, ohg, osb, wo_bf, g1, g2, wq_bf)


def _mem_kv_body(m_ref, g_ref, wk_ref, wv_ref, mk_ref, mv_ref):
    mn = _rms(m_ref[...], g_ref[...]).astype(bf16)
    mk_ref[...] = _dot(mn, wk_ref[...])
    mv_ref[...] = _dot(mn, wv_ref[...])


def _mem_kv(mem, g, wk_bf, wv_bf, tm):
    t = mem.shape[0]
    row = pl.BlockSpec((tm, D_MODEL), lambda i: (i, 0))
    return pl.pallas_call(
        _mem_kv_body,
        grid=(t // tm,),
        in_specs=[row, _const_spec((1, D_MODEL)), _const_spec(wk_bf.shape), _const_spec(wv_bf.shape)],
        out_specs=[row, row],
        out_shape=[jax.ShapeDtypeStruct((t, D_MODEL), f32)] * 2,
        compiler_params=_cparams(("arbitrary",)),
        name="mem_kv",
    )(mem, g, wk_bf, wv_bf)


def _cross_heads(q, mk_head, mv_head):
    scores = [_dot_nt(q[:, h * CA_DH:(h + 1) * CA_DH], mk_head(h)) for h in range(CA_HEADS)]
    probs = []
    for s in scores:
        e = jnp.exp(s - jnp.max(s, axis=-1, keepdims=True))
        probs.append((e / jnp.sum(e, axis=-1, keepdims=True)).astype(bf16))
    return jnp.concatenate([_dot(p, mv_head(h)) for h, p in enumerate(probs)], axis=1)


def _cross_prompt_body(q_ref, mk_ref, mv_ref, o_ref):
    mk = mk_ref[...].astype(bf16)
    mv = mv_ref[...].astype(bf16)
    o_ref[...] = _cross_heads(q_ref[...], lambda h: mk[:, h * CA_DH:(h + 1) * CA_DH],
                              lambda h: mv[:, h * CA_DH:(h + 1) * CA_DH]).astype(o_ref.dtype)


def _cross_prompt(qc, mk, mv, n_seq, seq_len, tm):
    nblk = seq_len // tm
    row = pl.BlockSpec((tm, D_MODEL), lambda b, j: (b * nblk + j, 0))
    mem = pl.BlockSpec((N_MEM, D_MODEL), lambda b, j: (b, 0))
    return pl.pallas_call(
        _cross_prompt_body,
        grid=(n_seq, nblk),
        in_specs=[row, mem, mem],
        out_specs=row,
        out_shape=jax.ShapeDtypeStruct(qc.shape, bf16),
        compiler_params=_cparams(("arbitrary", "arbitrary")),
        name="cross_prompt",
    )(qc, mk, mv)


def _cross_sample_body(q_ref, mk_hbm, mv_hbm, o_ref, kbuf, vbuf, sem, *, group, dec):
    i = pl.program_id(0)
    n_steps = pl.num_programs(0)

    def copies(step, slot):
        out = []
        for s in range(group):
            for h in range(CA_HEADS):
                n = step * group + s
                out.append(pltpu.make_async_copy(mk_hbm.at[n, :, h, :], kbuf.at[slot, s, h], sem.at[slot]))
                out.append(pltpu.make_async_copy(mv_hbm.at[n, :, h, :], vbuf.at[slot, s, h], sem.at[slot]))
        return out

    slot = i % 2

    @pl.when(i == 0)
    def _():
        for cp in copies(0, 0):
            cp.start()

    @pl.when(i + 1 < n_steps)
    def _():
        for cp in copies(i + 1, 1 - slot):
            cp.start()

    for cp in copies(i, slot):
        cp.wait()
    scores = [_dot_nt(q_ref[s * dec:(s + 1) * dec, h * CA_DH:(h + 1) * CA_DH], kbuf[slot, s, h].astype(bf16))
              for s in range(group) for h in range(CA_HEADS)]
    sc = jnp.concatenate(scores, axis=0)
    e = jnp.exp(sc - jnp.max(sc, axis=-1, keepdims=True))
    p = e / jnp.sum(e, axis=-1, keepdims=True)
    for s in range(group):
        outs = [_dot(p[(s * CA_HEADS + h) * dec:(s * CA_HEADS + h + 1) * dec, :].astype(bf16),
                     vbuf[slot, s, h].astype(bf16)) for h in range(CA_HEADS)]
        o_ref[s * dec:(s + 1) * dec, :] = jnp.concatenate(outs, axis=1).astype(o_ref.dtype)


def _cross_sample(qc, mk, mv, dec, group):
    n_seq = mk.shape[0]
    row = pl.BlockSpec((group * dec, D_MODEL), lambda i: (i, 0))
    buf = pltpu.VMEM((2, group, CA_HEADS, N_MEM, CA_DH), f32)
    return pl.pallas_call(
        functools.partial(_cross_sample_body, group=group, dec=dec),
        grid=(n_seq // group,),
        in_specs=[row, pl.BlockSpec(memory_space=pl.ANY), pl.BlockSpec(memory_space=pl.ANY)],
        out_specs=row,
        out_shape=jax.ShapeDtypeStruct(qc.shape, bf16),
        scratch_shapes=[buf, buf, pltpu.SemaphoreType.DMA((2,))],
        compiler_params=_cparams(("arbitrary",)),
        name="cross_sample",
    )(qc, mk, mv)


def _ca_out_body(ca_ref, x1_ref, wco_ref, g1_ref, g2_ref, x2_ref, h3_ref):
    x2 = x1_ref[...] + _rms(_dot(ca_ref[...], wco_ref[...]), g1_ref[...])
    x2_ref[...] = x2
    h3_ref[...] = _rms(x2, g2_ref[...]).astype(bf16)


def _ca_out(ca, x1, wco_bf, g1, g2, tm):
    t = x1.shape[0]
    row = pl.BlockSpec((tm, D_MODEL), lambda i: (i, 0))
    return pl.pallas_call(
        _ca_out_body,
        grid=(t // tm,),
        in_specs=[row, row, _const_spec(wco_bf.shape), _const_spec((1, D_MODEL)), _const_spec((1, D_MODEL))],
        out_specs=[row, row],
        out_shape=[jax.ShapeDtypeStruct((t, D_MODEL), f32), jax.ShapeDtypeStruct((t, D_MODEL), bf16)],
        compiler_params=_cparams(("arbitrary",)),
        name="ca_out",
    )(ca, x1, wco_bf, g1, g2)


FFN_CHUNK = 256


def _ffn_body(*refs, tm, dec, carry_tail):
    if carry_tail:
        h3_ref, x2_ref, wup_ref, cw_ref, cb_ref, wdn_ref, g_ref, y_ref, tail_ref, tail_scr, act_scr = refs
        halo_ref = None

        @pl.when(pl.program_id(1) == 0)
        def _():
            tail_scr[...] = jnp.zeros_like(tail_scr)
    else:
        h3_ref, x2_ref, halo_ref, wup_ref, cw_ref, cb_ref, wdn_ref, g_ref, y_ref, u_ref, act_scr = refs
        pos = lax.broadcasted_iota(jnp.int32, (tm, 1), 0) % dec
    h3 = h3_ref[...]

    def conv_chunk(col):
        cs = slice(col, col + FFN_CHUNK)
        u = _dot(h3, wup_ref[:, cs])
        if carry_tail:
            ext = jnp.concatenate([tail_scr[:, cs], u], axis=0)
            u1 = ext[7:7 + tm]
            u2 = ext[6:6 + tm]
            tail_scr[:, cs] = u[tm - 8:tm]
            tail_ref[:, cs] = u[tm - 8:tm]
        else:
            ext = jnp.concatenate([jnp.zeros((8, FFN_CHUNK), f32), u], axis=0)
            hal = halo_ref[:, cs]
            hext = jnp.concatenate([hal, jnp.zeros((8, FFN_CHUNK), f32)], axis=0)
            u1 = jnp.where(pos == 0, hext[1:1 + tm], ext[7:7 + tm])
            u2 = jnp.where(pos < 2, hal, ext[6:6 + tm])
            u_ref[:, cs] = u
        return cb_ref[:, cs] + cw_ref[2:3, cs] * u + cw_ref[1:2, cs] * u1 + cw_ref[0:1, cs] * u2

    for cj in range(D_FF // FFN_CHUNK):
        gate = conv_chunk(cj * FFN_CHUNK)
        val = conv_chunk(D_FF + cj * FFN_CHUNK)
        act_scr[:, cj * FFN_CHUNK:(cj + 1) * FFN_CHUNK] = (jax.nn.gelu(gate, approximate=True) * val).astype(bf16)
    y = _dot(act_scr[...], wdn_ref[...])
    y_ref[...] = x2_ref[...] + _rms(y, g_ref[...])


def _ffn_prompt(h3, x2, wup_bf, cw, cb, wdn_bf, g, n_seq, seq_len, tm):
    nblk = seq_len // tm
    row = pl.BlockSpec((tm, D_MODEL), lambda b, j: (b * nblk + j, 0))
    return pl.pallas_call(
        functools.partial(_ffn_body, tm=tm, dec=None, carry_tail=True),
        grid=(n_seq, nblk),
        in_specs=[row, row, _const_spec(wup_bf.shape), _const_spec(cw.shape), _const_spec(cb.shape),
                  _const_spec(wdn_bf.shape), _const_spec((1, D_MODEL))],
        out_specs=[row, pl.BlockSpec((8, 2 * D_FF), lambda b, j: (b, 0))],
        out_shape=[jax.ShapeDtypeStruct((n_seq * seq_len, D_MODEL), f32),
                   jax.ShapeDtypeStruct((n_seq * 8, 2 * D_FF), f32)],
        scratch_shapes=[pltpu.VMEM((8, 2 * D_FF), f32), pltpu.VMEM((tm, D_FF), bf16)],
        compiler_params=_cparams(("arbitrary", "arbitrary")),
        name="ffn_prompt",
    )(h3, x2, wup_bf, cw, cb, wdn_bf, g)


def _ffn_sample(h3, x2, halo, wup_bf, cw, cb, wdn_bf, g, dec, tm):
    t = h3.shape[0]
    row = pl.BlockSpec((tm, D_MODEL), lambda i: (i, 0))
    wide = pl.BlockSpec((tm, 2 * D_FF), lambda i: (i, 0))
    return pl.pallas_call(
        functools.partial(_ffn_body, tm=tm, dec=dec, carry_tail=False),
        grid=(t // tm,),
        in_specs=[row, row, wide, _const_spec(wup_bf.shape), _const_spec(cw.shape), _const_spec(cb.shape),
                  _const_spec(wdn_bf.shape), _const_spec((1, D_MODEL))],
        out_specs=[row, wide],
        out_shape=[jax.ShapeDtypeStruct((t, D_MODEL), f32), jax.ShapeDtypeStruct((t, 2 * D_FF), f32)],
        scratch_shapes=[pltpu.VMEM((tm, D_FF), bf16)],
        compiler_params=_cparams(("arbitrary",)),
        name="ffn_sample",
    )(h3, x2, halo, wup_bf, cw, cb, wdn_bf, g)


def kernel(x_prompt, x_sample, cache_sb_k, cache_sb_v, state_hgrn, state_ffn_conv, cache_mem_k, cache_mem_v,
           page_table, mem_prompt, w_in, hg_norm, hg_lb, sb_bias, w_o, g_mix_pre, g_mix_post, g_ca_pre, g_ca_post,
           g_mem, w_cq, w_ck, w_cv, w_co, g_ffn_pre, g_ffn_post, w_up, conv_w, conv_b, w_down):
    n_p, seq_len, _ = x_prompt.shape
    n_d, dec, _ = x_sample.shape
    depth = w_in.shape[0]
    assert depth == 1, "single-layer step"
    assert dec >= 2 and dec % 8 == 0, "the conv tail is taken from the new rows"
    l = 0
    row = lambda a: a[l].reshape(1, -1)
    w_in_bf, w_o_bf, w_cq_bf, w_co_bf = (w[l].astype(bf16) for w in (w_in, w_o, w_cq, w_co))
    w_ck_bf, w_cv_bf, w_up_bf, w_dn_bf = (w[l].astype(bf16) for w in (w_ck, w_cv, w_up, w_down))
    lbp = hg_lb[l:l + 2]
    gn = row(hg_norm)
    bias = sb_bias[l]
    cb = row(conv_b)
    cw = conv_w[l]

    xp = x_prompt.reshape(n_p * seq_len, D_MODEL)
    xs = x_sample.reshape(n_d * dec, D_MODEL)

    hin, sk_t, sv_t, qn, kb, vb = _in_proj(xp, row(g_mix_pre), w_in_bf, 512, n_seq=n_p)
    ohg, s_p = _hgrn_prompt(hin, lbp, gn, n_p, seq_len, 256, 16)
    osb = _sb_prompt(qn, kb, vb, bias, n_p, seq_len, 256)
    x1, qc = _mix_out(xp, ohg, osb, w_o_bf, row(g_mix_post), row(g_ca_pre), w_cq_bf, 512)
    mk_p, mv_p = _mem_kv(mem_prompt.reshape(n_p * N_MEM, D_MODEL), row(g_mem), w_ck_bf, w_cv_bf, 256)
    ca = _cross_prompt(qc, mk_p, mv_p, n_p, seq_len, 512)
    x2, h3 = _ca_out(ca, x1, w_co_bf, row(g_ca_post), row(g_ffn_pre), 512)
    yp, tail_p = _ffn_prompt(h3, x2, w_up_bf, cw, cb, w_dn_bf, row(g_ffn_post), n_p, seq_len, 256)

    ts = n_d * dec
    hin_s, sk_s, sv_s, qn_s, kb_s, vb_s = _in_proj(xs, row(g_mix_pre), w_in_bf, 512)
    ohg_s, s_s = _hgrn_sample(hin_s, lbp, gn, state_hgrn[l], dec, 16)
    n_pool = cache_sb_k.shape[1]
    pages_t = lambda c: jnp.transpose(c[l], (0, 2, 3, 1)).reshape(n_pool, SB_W, PAGE)
    osb_s = _sb_sample(qn_s, kb_s, vb_s, pages_t(cache_sb_k), pages_t(cache_sb_v), page_table, bias,
                       dec).reshape(ts, SB_W)
    x1_s, qc_s = _mix_out(xs, ohg_s, osb_s, w_o_bf, row(g_mix_post), row(g_ca_pre), w_cq_bf, 512)
    ca_s = _cross_sample(qc_s, cache_mem_k[l], cache_mem_v[l], dec, 4)
    x2_s, h3_s = _ca_out(ca_s, x1_s, w_co_bf, row(g_ca_post), row(g_ffn_pre), 512)
    halo = jnp.pad(state_ffn_conv[l], ((0, 0), (0, dec - 2), (0, 0))).reshape(ts, 2 * D_FF)
    ys, u_s = _ffn_sample(h3_s, x2_s, halo, w_up_bf, cw, cb, w_dn_bf, row(g_ffn_post), dec, 128)

    kv_out = lambda a: jnp.transpose(a.reshape(n_p, SB_HEADS, SB_DH, seq_len), (0, 3, 1, 2))[None]
    return (yp.reshape(n_p, seq_len, D_MODEL), ys.reshape(n_d, dec, D_MODEL),
            kv_out(sk_t), kv_out(sv_t),
            s_p[None],
            tail_p.reshape(n_p, 8, 2 * D_FF)[None, :, 6:8],
            mk_p.reshape(1, n_p, N_MEM, CA_HEADS, CA_DH), mv_p.reshape(1, n_p, N_MEM, CA_HEADS, CA_DH),
            sk_s.reshape(1, n_d, dec, SB_HEADS, SB_DH), sv_s.reshape(1, n_d, dec, SB_HEADS, SB_DH),
            s_s[None],
            u_s.reshape(n_d, dec, 2 * D_FF)[None, :, dec - 2:dec])
```

```python
import functools

import numpy as np
import jax
import jax.numpy as jnp
from jax import lax
from jax.experimental import pallas as pl
from jax.experimental.pallas import tpu as pltpu

f32 = jnp.float32
bf16 = jnp.bfloat16

D_MODEL = 1024
HG_HEADS = 8
HG_DK = 64
HG_W = HG_HEADS * HG_DK
SB_HEADS = 8
SB_DH = 64
SB_W = SB_HEADS * SB_DH
SB_SCALE = SB_DH ** -0.5
LOG2E = 1.4426950408889634
N_MEM = 256
CA_HEADS = 4
CA_DH = D_MODEL // CA_HEADS
CA_SCALE = CA_DH ** -0.5
D_FF = 2816
RMS_EPS = 1e-6
PAGE = 128
LANES = 128
SUB = 8
VMEM_LIMIT = 56 * 1024 * 1024


def _cparams(sem, flags=None):
    return pltpu.CompilerParams(dimension_semantics=sem, vmem_limit_bytes=VMEM_LIMIT, flags=flags)


def _const_spec(shape):
    nd = len(shape)
    return pl.BlockSpec(shape, lambda *_: (0,) * nd, pipeline_mode=pl.Buffered(1))


def _rms(x, g):
    return x * lax.rsqrt(jnp.mean(x * x, axis=-1, keepdims=True) + RMS_EPS) * g


def _dot(a, b):
    return jnp.dot(a, b, preferred_element_type=f32)


def _dot_nt(a, b):
    return lax.dot_general(a, b, (((1,), (1,)), ((), ())), preferred_element_type=f32)


def _dot_tn(a, b):
    return lax.dot_general(a, b, (((0,), (0,)), ((), ())), preferred_element_type=f32)


def _in_proj_body(x_ref, g_ref, w_ref, hin_ref, sk_ref, sv_ref, qn_ref, kb_ref, vb_ref, *, kv_transposed):
    xn = _rms(x_ref[...], g_ref[...]).astype(bf16)
    for j in range(4):
        hin_ref[:, j * HG_W:(j + 1) * HG_W] = _dot(xn, w_ref[:, j * HG_W:(j + 1) * HG_W])
    base = 4 * HG_W
    q = _dot(xn, w_ref[:, base:base + SB_W])
    qn_ref[...] = (q * (-SB_SCALE * LOG2E)).astype(bf16)
    k = _dot(xn, w_ref[:, base + SB_W:base + 2 * SB_W])
    kb_ref[...] = k.astype(bf16)
    v = _dot(xn, w_ref[:, base + 2 * SB_W:base + 3 * SB_W])
    vb_ref[...] = v.astype(bf16)
    if kv_transposed:
        sk_ref[0] = k.T
        sv_ref[0] = v.T
    else:
        sk_ref[...] = k
        sv_ref[...] = v


def _in_proj(x, g, w_bf, tm, n_seq=None):
    t = x.shape[0]
    d_in = w_bf.shape[1]
    row = lambda n: pl.BlockSpec((tm, n), lambda i: (i, 0))
    if n_seq is None:
        kv_spec, kv_shape = row(SB_W), jax.ShapeDtypeStruct((t, SB_W), f32)
    else:
        nblk = t // n_seq // tm
        kv_spec = pl.BlockSpec((1, SB_W, tm), lambda i: (i // nblk, 0, i % nblk))
        kv_shape = jax.ShapeDtypeStruct((n_seq, SB_W, t // n_seq), f32)
    return pl.pallas_call(
        functools.partial(_in_proj_body, kv_transposed=n_seq is not None),
        grid=(t // tm,),
        in_specs=[row(D_MODEL), _const_spec((1, D_MODEL)), _const_spec((D_MODEL, d_in))],
        out_specs=[row(4 * HG_W), kv_spec, kv_spec, row(SB_W), row(SB_W), row(SB_W)],
        out_shape=[jax.ShapeDtypeStruct((t, 4 * HG_W), f32), kv_shape, kv_shape,
                   jax.ShapeDtypeStruct((t, SB_W), bf16), jax.ShapeDtypeStruct((t, SB_W), bf16),
                   jax.ShapeDtypeStruct((t, SB_W), bf16)],
        compiler_params=_cparams(("arbitrary",)),
        name="in_proj",
    )(x, g, w_bf)


def _hgrn_consts(nb, c):
    r = np.arange(nb)
    same = (r[:, None] // c) == (r[None, :] // c)
    tri = same & (r[None, :] <= r[:, None])
    stack = np.concatenate([tri, same], axis=0).astype(np.float32)
    h = np.arange(HG_W) // HG_DK
    bo = (h[:, None] == h[None, :]).astype(np.float32)
    return jnp.asarray(stack, bf16), jnp.asarray(bo, bf16)


def _hgrn_prep(hin_ref, lbp_ref, stack_ref, scr, nb):
    q_scr, e_scr, v_scr, b_scr, qe_scr, kt_scr, el_scr = scr
    a = lbp_ref[...]
    e = jnp.exp(a - jnp.max(a, axis=0, keepdims=True))
    lb = e[0:1] / jnp.sum(e, axis=0, keepdims=True)
    hq = hin_ref[:, 0:HG_W]
    f = lb + (1.0 - lb) * jax.nn.sigmoid(hin_ref[:, HG_W:2 * HG_W])
    g = jnp.log2(f)
    kk = 1.0 - f
    g1 = g.astype(bf16)
    r1 = g - g1.astype(f32)
    g2 = r1.astype(bf16)
    g3 = (r1 - g2.astype(f32)).astype(bf16)
    st = stack_ref[...]
    bb = _dot(st, g1) + _dot(st, g2) + _dot(st, g3)
    b = bb[:nb]
    btot = bb[nb:]
    q_scr[...] = hq
    e_scr[...] = b - jnp.log2(kk)
    v_scr[...] = hin_ref[:, 2 * HG_W:3 * HG_W]
    b_scr[...] = b
    qe_scr[...] = hq * jnp.exp2(b)
    kt_scr[...] = kk * jnp.exp2(btot - b)
    el_scr[...] = jnp.exp2(btot)


def _hgrn_chunk(scr, bo, r0, c, st_list):
    q_scr, e_scr, v_scr, b_scr, qe_scr, kt_scr, el_scr = scr
    rows = pl.ds(r0, c)
    b_c = b_scr[rows, :]
    q_c = q_scr[rows, :]
    e_c = e_scr[rows, :]
    v_c = v_scr[rows, :]
    qe_c = qe_scr[rows, :].astype(bf16)
    kt_c = kt_scr[rows, :].astype(bf16)
    el_c = el_scr[pl.ds(r0, 1), :]
    v_cb = v_c.astype(bf16)
    n_sub = c // SUB
    t_loc = lax.broadcasted_iota(jnp.int32, (SUB, 1), 0)
    pieces, index = [], {}
    for s in range(c):
        for tg in range(s // SUB, n_sub):
            sl = slice(tg * SUB, (tg + 1) * SUB)
            pc = q_c[sl, :] * jnp.exp2(b_c[sl, :] - e_c[s:s + 1, :])
            if tg == s // SUB:
                pc = jnp.where(t_loc >= s % SUB, pc, 0.0)
            index[(s, tg)] = len(pieces)
            pieces.append(pc)
    p = jnp.concatenate(pieces, axis=0).astype(bf16)
    pw = _dot(p, bo)
    groups = []
    for tg in range(n_sub):
        acc = None
        for s in range((tg + 1) * SUB):
            k = index[(s, tg)]
            term = pw[k * SUB:(k + 1) * SUB, :] * v_c[s:s + 1, :]
            acc = term if acc is None else acc + term
        groups.append(acc)
    o = jnp.concatenate(groups, axis=0)
    ri = lax.broadcasted_iota(jnp.int32, (LANES, LANES), 0) // HG_DK
    ci = lax.broadcasted_iota(jnp.int32, (LANES, LANES), 1) // HG_DK
    same_head = ri == ci
    o_parts, new_states = [], []
    for pr in range(HG_W // LANES):
        sl = slice(pr * LANES, (pr + 1) * LANES)
        st = st_list[pr]
        o_parts.append(_dot_nt(qe_c[:, sl], st.astype(bf16)))
        upd = _dot_tn(v_cb[:, sl], kt_c[:, sl])
        new_states.append(st * el_c[:, sl] + jnp.where(same_head, upd, 0.0))
    return o + jnp.concatenate(o_parts, axis=1), new_states


def _hgrn_finish(o, hin_ref, gn_ref, bo, o_ref):
    ms = _dot((o * o).astype(bf16), bo) * (1.0 / HG_DK)
    gate = hin_ref[:, 3 * HG_W:4 * HG_W]
    o_ref[...] = (o * lax.rsqrt(ms + RMS_EPS) * gn_ref[...] * (gate * jax.nn.sigmoid(gate))).astype(o_ref.dtype)


def _pair_state_out(st):
    t = st.T
    return t[0:HG_DK, 0:HG_DK], t[HG_DK:LANES, HG_DK:LANES]


def _hgrn_prompt_body(hin_ref, lbp_ref, gn_ref, stack_ref, bo_ref, o_ref, sout_ref,
                      st_scr, oacc_scr, *scr, nb, c):
    j = pl.program_id(1)

    @pl.when(j == 0)
    def _():
        st_scr[...] = jnp.zeros_like(st_scr)

    _hgrn_prep(hin_ref, lbp_ref, stack_ref, scr, nb)
    bo = bo_ref[...]

    def step(ci, carry):
        r0 = pl.multiple_of(ci * c, c)
        o, new_states = _hgrn_chunk(scr, bo, r0, c, [st_scr[pr] for pr in range(HG_W // LANES)])
        for pr, st in enumerate(new_states):
            st_scr[pr] = st
        oacc_scr[pl.ds(r0, c), :] = o
        return carry

    lax.fori_loop(0, nb // c, step, 0, unroll=2)
    _hgrn_finish(oacc_scr[...], hin_ref, gn_ref, bo, o_ref)

    @pl.when(j == pl.num_programs(1) - 1)
    def _():
        for pr in range(HG_W // LANES):
            sa, sb = _pair_state_out(st_scr[pr])
            sout_ref[0, 2 * pr] = sa
            sout_ref[0, 2 * pr + 1] = sb


def _hgrn_prompt(hin, lbp, gn, n_seq, seq_len, nb, c):
    stack, bo = _hgrn_consts(nb, c)
    nblk = seq_len // nb
    scr = [pltpu.VMEM((nb, HG_W), f32) for _ in range(7)]
    return pl.pallas_call(
        functools.partial(_hgrn_prompt_body, nb=nb, c=c),
        grid=(n_seq, nblk),
        in_specs=[pl.BlockSpec((nb, 4 * HG_W), lambda b, j: (b * nblk + j, 0)),
                  _const_spec(lbp.shape), _const_spec((1, HG_W)),
                  _const_spec(stack.shape), _const_spec(bo.shape)],
        out_specs=[pl.BlockSpec((nb, HG_W), lambda b, j: (b * nblk + j, 0)),
                   pl.BlockSpec((1, HG_HEADS, HG_DK, HG_DK), lambda b, j: (b, 0, 0, 0))],
        out_shape=[jax.ShapeDtypeStruct((n_seq * seq_len, HG_W), bf16),
                   jax.ShapeDtypeStruct((n_seq, HG_HEADS, HG_DK, HG_DK), f32)],
        scratch_shapes=[pltpu.VMEM((HG_W // LANES, LANES, LANES), f32), pltpu.VMEM((nb, HG_W), f32)] + scr,
        compiler_params=_cparams(("arbitrary", "arbitrary")),
        name="hgrn_prompt",
    )(hin, lbp, gn, stack, bo)


def _hgrn_sample_body(hin_ref, lbp_ref, gn_ref, stack_ref, bo_ref, s0_ref, o_ref, sout_ref,
                      oacc_scr, *scr, nb, c):
    _hgrn_prep(hin_ref, lbp_ref, stack_ref, scr, nb)
    bo = bo_ref[...]
    zero = jnp.zeros((HG_DK, HG_DK), f32)

    def step(ci, carry):
        r0 = pl.multiple_of(ci * c, c)
        states = []
        for pr in range(HG_W // LANES):
            sa = s0_ref[ci, 2 * pr]
            sb = s0_ref[ci, 2 * pr + 1]
            bd = jnp.concatenate([jnp.concatenate([sa, zero], axis=1),
                                  jnp.concatenate([zero, sb], axis=1)], axis=0)
            states.append(bd.T)
        o, new_states = _hgrn_chunk(scr, bo, r0, c, states)
        for pr, st in enumerate(new_states):
            sa, sb = _pair_state_out(st)
            sout_ref[ci, 2 * pr] = sa
            sout_ref[ci, 2 * pr + 1] = sb
        oacc_scr[pl.ds(r0, c), :] = o
        return carry

    lax.fori_loop(0, nb // c, step, 0, unroll=2)
    _hgrn_finish(oacc_scr[...], hin_ref, gn_ref, bo, o_ref)


def _hgrn_sample(hin, lbp, gn, s0, c, seqs_per_step):
    n_seq = s0.shape[0]
    nb = seqs_per_step * c
    stack, bo = _hgrn_consts(nb, c)
    scr = [pltpu.VMEM((nb, HG_W), f32) for _ in range(7)]
    st_spec = pl.BlockSpec((seqs_per_step, HG_HEADS, HG_DK, HG_DK), lambda i: (i, 0, 0, 0))
    return pl.pallas_call(
        functools.partial(_hgrn_sample_body, nb=nb, c=c),
        grid=(n_seq // seqs_per_step,),
        in_specs=[pl.BlockSpec((nb, 4 * HG_W), lambda i: (i, 0)),
                  _const_spec(lbp.shape), _const_spec((1, HG_W)),
                  _const_spec(stack.shape), _const_spec(bo.shape), st_spec],
        out_specs=[pl.BlockSpec((nb, HG_W), lambda i: (i, 0)), st_spec],
        out_shape=[jax.ShapeDtypeStruct((n_seq * c, HG_W), bf16),
                   jax.ShapeDtypeStruct(s0.shape, f32)],
        scratch_shapes=[pltpu.VMEM((nb, HG_W), f32)] + scr,
        compiler_params=_cparams(("arbitrary",)),
        name="hgrn_sample",
    )(hin, lbp, gn, stack, bo, s0)


NEG_BIG = -1e30


def _sb_logs(zp, r, strict):
    c = jnp.minimum(zp, 0.0) - jnp.log2(1.0 + jnp.exp2(-jnp.abs(zp)))
    ls = (c - zp) + r
    if strict is not None:
        c = jnp.where(strict, c, 0.0)
        ls = jnp.where(strict, ls, NEG_BIG)
    return c.astype(bf16), ls, r + jnp.sum(c, axis=-1, keepdims=True)


def _sb_weights(ls, later):
    return jnp.exp2(ls + later).astype(bf16)


def _sb_prompt_body(bias_ref, q_ref, k_ref, v_ref, u_ref, o_ref,
                    kx_scr, qs_scr, zp_scr, c_scr, ls_scr, loc_scr, a_scr, acc_scr, r_scr, *, tq, tk):
    pr = pl.program_id(1)
    i = pl.program_id(2)
    nk = pl.num_programs(2) * (tq // tk)
    lane = lax.broadcasted_iota(jnp.int32, (1, LANES), 1)

    @pl.when(i == 0)
    def _():
        kx_scr[:, 0:LANES] = k_ref[...]
        kx_scr[:, LANES:2 * LANES] = jnp.broadcast_to(jnp.where(lane < 2, 1.0, 0.0), (kx_scr.shape[0], LANES)
                                                      ).astype(bf16)

    q = q_ref[...]
    zero_q = jnp.zeros_like(q)
    qs_scr[0:tq, 0:LANES] = jnp.where(lane < SB_DH, q, zero_q)
    qs_scr[tq:2 * tq, 0:LANES] = jnp.where(lane >= SB_DH, q, zero_q)
    row = lax.broadcasted_iota(jnp.int32, (2 * tq, 1), 0)
    nbias = jnp.where(row < tq, -LOG2E * bias_ref[2 * pr], -LOG2E * bias_ref[2 * pr + 1])
    nb_hi = nbias.astype(bf16).astype(f32)
    qs_scr[:, LANES:2 * LANES] = jnp.where(lane == 0, nb_hi, jnp.where(lane == 1, nbias - nb_hi, 0.0)).astype(bf16)
    t_row = lax.broadcasted_iota(jnp.int32, (2 * tq, tk), 0) % tq
    s_col = lax.broadcasted_iota(jnp.int32, (2 * tq, tk), 1)
    top = 2 * i + 1

    def kblock(ref, kb):
        return ref[pl.ds(pl.multiple_of(kb * tk, tk), tk), :]

    def scores(kb):
        return _dot_nt(qs_scr[...], kblock(kx_scr, jnp.maximum(kb, 0)))

    c0, ls0, r0 = _sb_logs(scores(top), jnp.zeros((2 * tq, 1), f32), s_col + tk < t_row)
    c_scr[0] = c0
    ls_scr[0] = ls0
    r_scr[...] = r0
    zp_scr[1] = scores(top - 1)
    a_scr[...] = jnp.zeros_like(a_scr)
    acc_scr[...] = jnp.zeros_like(acc_scr)

    def step(n, slot, strict):
        prev = 1 - slot
        acc_scr[...] += _dot(a_scr[...], kblock(v_ref, jnp.minimum(top - n + 2, nk - 1)))
        loc_scr[...] = _dot(c_scr[prev], u_ref[...])
        zp_scr[prev] = scores(top - n - 1)
        c, ls, r = _sb_logs(zp_scr[slot], r_scr[...], strict)
        c_scr[slot] = c
        ls_scr[slot] = ls
        r_scr[...] = r
        a_scr[...] = _sb_weights(ls_scr[prev], loc_scr[...])

    step(1, 1, s_col < t_row)

    def pair(j, carry):
        step(2 * j + 2, 0, None)
        step(2 * j + 3, 1, None)
        return carry

    lax.fori_loop(0, i, pair, 0)
    acc = acc_scr[...] + _dot(a_scr[...], kblock(v_ref, 1))
    a_last = _sb_weights(ls_scr[1], _dot(c_scr[1], u_ref[...]))
    acc = acc + _dot(a_last, kblock(v_ref, 0))
    o_ref[...] = jnp.where(lane < SB_DH, acc[:tq], acc[tq:]).astype(o_ref.dtype)


def _strict_upper(n):
    r = np.arange(n)
    return jnp.asarray((r[:, None] > r[None, :]).astype(np.float32), bf16)


def _sb_prompt(qn, kb, vb, bias, n_seq, seq_len, tk):
    tq = 2 * tk
    nq = seq_len // tq
    u = _strict_upper(tk)
    return pl.pallas_call(
        functools.partial(_sb_prompt_body, tq=tq, tk=tk),
        grid=(n_seq, SB_W // LANES, nq),
        in_specs=[pl.BlockSpec(memory_space=pltpu.SMEM),
                  pl.BlockSpec((tq, LANES), lambda b, p, i: (b * nq + i, p)),
                  pl.BlockSpec((seq_len, LANES), lambda b, p, i: (b, p)),
                  pl.BlockSpec((seq_len, LANES), lambda b, p, i: (b, p)),
                  _const_spec(u.shape)],
        out_specs=pl.BlockSpec((tq, LANES), lambda b, p, i: (b * nq + i, p)),
        out_shape=jax.ShapeDtypeStruct((n_seq * seq_len, SB_W), bf16),
        scratch_shapes=[pltpu.VMEM((seq_len, 2 * LANES), bf16), pltpu.VMEM((2 * tq, 2 * LANES), bf16),
                        pltpu.VMEM((2, 2 * tq, tk), f32), pltpu.VMEM((2, 2 * tq, tk), bf16),
                        pltpu.VMEM((2, 2 * tq, tk), f32), pltpu.VMEM((2 * tq, tk), f32),
                        pltpu.VMEM((2 * tq, tk), bf16), pltpu.VMEM((2 * tq, LANES), f32),
                        pltpu.VMEM((2 * tq, 1), f32)],
        compiler_params=_cparams(("arbitrary", "arbitrary", "arbitrary")),
        name="sb_prompt",
    )(bias, qn, kb, vb, u)


def _sb_sample_body(pt_ref, bias_ref, q_ref, kn_ref, vn_ref, u_ref, *rest, n_pages, dec):
    k_refs = rest[:n_pages]
    v_refs = rest[n_pages:2 * n_pages]
    o_ref = rest[2 * n_pages]
    rows = SB_HEADS * dec
    lane_head = lax.broadcasted_iota(jnp.int32, (1, SB_W), 1) // SB_DH
    q = q_ref[0].astype(f32)
    qbd = jnp.concatenate([jnp.where(lane_head == h, q, 0.0) for h in range(SB_HEADS)], axis=0).astype(bf16)
    row_head = lax.broadcasted_iota(jnp.int32, (rows, 1), 0) // dec
    nbias = jnp.zeros((rows, 1), f32)
    for h in range(SB_HEADS):
        nbias = jnp.where(row_head == h, -LOG2E * bias_ref[h], nbias)
    pad = jnp.zeros((PAGE - dec, SB_W), f32)
    kn = jnp.concatenate([kn_ref[0].astype(f32), pad], axis=0).astype(bf16)
    vn = jnp.concatenate([vn_ref[0].astype(f32), pad], axis=0).astype(bf16)
    t_row = lax.broadcasted_iota(jnp.int32, (rows, PAGE), 0) % dec
    s_col = lax.broadcasted_iota(jnp.int32, (rows, PAGE), 1)

    def pair(refs, g):
        return jnp.concatenate([refs[2 * g][0], refs[2 * g + 1][0]], axis=1).astype(bf16)

    groups = list(range(n_pages // 2 - 1, -1, -1))
    scores = [_dot_nt(qbd, kn)] + [_dot(qbd, pair(k_refs, g)) for g in groups]
    r = jnp.zeros((rows, 1), f32)
    cs, lss = [], []
    for n, zp in enumerate(scores):
        c, ls, r = _sb_logs(zp + nbias, r, (s_col < t_row) if n == 0 else None)
        cs.append(c)
        lss.append(ls)
    later = [_dot(cs[0], u_ref[0:PAGE, 0:PAGE])] + [_dot(c, u_ref[...]) for c in cs[1:]]
    acc = _dot(_sb_weights(lss[0], later[0]), vn)
    for n, g in enumerate(groups):
        acc = acc + _dot_nt(_sb_weights(lss[n + 1], later[n + 1]), pair(v_refs, g))
    out = jnp.zeros((dec, SB_W), f32)
    for h in range(SB_HEADS):
        out = out + jnp.where(lane_head == h, acc[h * dec:(h + 1) * dec, :], 0.0)
    o_ref[0] = out.astype(o_ref.dtype)


def _sb_sample(qn, kb, vb, cache_k, cache_v, page_table, bias, dec):
    n_seq, n_pages = page_table.shape
    assert n_pages % 2 == 0, "pages are consumed two at a time"
    u = _strict_upper(2 * PAGE)
    pt = page_table.reshape(-1)
    tok = pl.BlockSpec((1, dec, SB_W), lambda n, pt: (n, 0, 0))

    def page_spec(j):
        return pl.BlockSpec((1, SB_W, PAGE), lambda n, pt: (pt[n * n_pages + j], 0, 0))

    grid_spec = pltpu.PrefetchScalarGridSpec(
        num_scalar_prefetch=1,
        grid=(n_seq,),
        in_specs=[pl.BlockSpec(memory_space=pltpu.SMEM), tok, tok, tok,
                  pl.BlockSpec(u.shape, lambda n, pt: (0, 0))]
                 + [page_spec(j) for j in range(n_pages)] * 2,
        out_specs=tok,
    )
    return pl.pallas_call(
        functools.partial(_sb_sample_body, n_pages=n_pages, dec=dec),
        grid_spec=grid_spec,
        out_shape=jax.ShapeDtypeStruct((n_seq, dec, SB_W), bf16),
        compiler_params=_cparams(("arbitrary",)),
        name="sb_sample",
    )(pt, bias, qn.reshape(n_seq, dec, SB_W), kb.reshape(n_seq, dec, SB_W), vb.reshape(n_seq, dec, SB_W), u,
      *([cache_k] * n_pages), *([cache_v] * n_pages))


def _mix_out_body(x_ref, ohg_ref, osb_ref, wo_ref, g1_ref, g2_ref, wq_ref, x1_ref, qc_ref):
    mixed = _dot(ohg_ref[...], wo_ref[0:HG_W, :]) + _dot(osb_ref[...], wo_ref[HG_W:HG_W + SB_W, :])
    x1 = x_ref[...] + _rms(mixed, g1_ref[...])
    x1_ref[...] = x1
    h2 = _rms(x1, g2_ref[...]).astype(bf16)
    qc_ref[...] = (_dot(h2, wq_ref[...]) * CA_SCALE).astype(bf16)


def _mix_out(x, ohg, osb, wo_bf, g1, g2, wq_bf, tm):
    t = x.shape[0]
    row = lambda n: pl.BlockSpec((tm, n), lambda i: (i, 0))
    return pl.pallas_call(
        _mix_out_body,
        grid=(t // tm,),
        in_specs=[row(D_MODEL), row(HG_W), row(SB_W), _const_spec(wo_bf.shape),
                  _const_spec((1, D_MODEL)), _const_spec((1, D_MODEL)), _const_spec(wq_bf.shape)],
        out_specs=[row(D_MODEL), row(D_MODEL)],
        out_shape=[jax.ShapeDtypeStruct((t, D_MODEL), f32), jax.ShapeDtypeStruct((t, D_MODEL), bf16)],
        compiler_params=_cparams(("arbitrary",)),
        name="mix_out",
    )(x, ohg, osb, wo_bf, g1, g2, wq_bf)


def _mem_kv_body(m_ref, g_ref, wk_ref, wv_ref, mk_ref, mv_ref):
    mn = _rms(m_ref[...], g_ref[...]).astype(bf16)
    mk_ref[...] = _dot(mn, wk_ref[...])
    mv_ref[...] = _dot(mn, wv_ref[...])


def _mem_kv(mem, g, wk_bf, wv_bf, tm):
    t = mem.shape[0]
    row = pl.BlockSpec((tm, D_MODEL), lambda i: (i, 0))
    return pl.pallas_call(
        _mem_kv_body,
        grid=(t // tm,),
        in_specs=[row, _const_spec((1, D_MODEL)), _const_spec(wk_bf.shape), _const_spec(wv_bf.shape)],
        out_specs=[row, row],
        out_shape=[jax.ShapeDtypeStruct((t, D_MODEL), f32)] * 2,
        compiler_params=_cparams(("arbitrary",)),
        name="mem_kv",
    )(mem, g, wk_bf, wv_bf)


def _cross_heads(q, mk_head, mv_head):
    scores = [_dot_nt(q[:, h * CA_DH:(h + 1) * CA_DH], mk_head(h)) for h in range(CA_HEADS)]
    probs = []
    for s in scores:
        e = jnp.exp(s - jnp.max(s, axis=-1, keepdims=True))
        probs.append((e / jnp.sum(e, axis=-1, keepdims=True)).astype(bf16))
    return jnp.concatenate([_dot(p, mv_head(h)) for h, p in enumerate(probs)], axis=1)


def _cross_sample_body(q_ref, mk_hbm, mv_hbm, o_ref, kbuf, vbuf, sem, *, group, dec):
    i = pl.program_id(0)
    n_steps = pl.num_programs(0)

    def copies(step, slot):
        out = []
        for s in range(group):
            for h in range(CA_HEADS):
                n = step * group + s
                out.append(pltpu.make_async_copy(mk_hbm.at[n, :, h, :], kbuf.at[slot, s, h], sem.at[slot]))
                out.append(pltpu.make_async_copy(mv_hbm.at[n, :, h, :], vbuf.at[slot, s, h], sem.at[slot]))
        return out

    slot = i % 2

    @pl.when(i == 0)
    def _():
        for cp in copies(0, 0):
            cp.start()

    @pl.when(i + 1 < n_steps)
    def _():
        for cp in copies(i + 1, 1 - slot):
            cp.start()

    for cp in copies(i, slot):
        cp.wait()
    scores = [_dot_nt(q_ref[s * dec:(s + 1) * dec, h * CA_DH:(h + 1) * CA_DH], kbuf[slot, s, h].astype(bf16))
              for s in range(group) for h in range(CA_HEADS)]
    sc = jnp.concatenate(scores, axis=0)
    e = jnp.exp(sc - jnp.max(sc, axis=-1, keepdims=True))
    p = e / jnp.sum(e, axis=-1, keepdims=True)
    for s in range(group):
        outs = [_dot(p[(s * CA_HEADS + h) * dec:(s * CA_HEADS + h + 1) * dec, :].astype(bf16),
                     vbuf[slot, s, h].astype(bf16)) for h in range(CA_HEADS)]
        o_ref[s * dec:(s + 1) * dec, :] = jnp.concatenate(outs, axis=1).astype(o_ref.dtype)


def _cross_sample(qc, mk, mv, dec, group):
    n_seq = mk.shape[0]
    row = pl.BlockSpec((group * dec, D_MODEL), lambda i: (i, 0))
    buf = pltpu.VMEM((2, group, CA_HEADS, N_MEM, CA_DH), f32)
    return pl.pallas_call(
        functools.partial(_cross_sample_body, group=group, dec=dec),
        grid=(n_seq // group,),
        in_specs=[row, pl.BlockSpec(memory_space=pl.ANY), pl.BlockSpec(memory_space=pl.ANY)],
        out_specs=row,
        out_shape=jax.ShapeDtypeStruct(qc.shape, bf16),
        scratch_shapes=[buf, buf, pltpu.SemaphoreType.DMA((2,))],
        compiler_params=_cparams(("arbitrary",)),
        name="cross_sample",
    )(qc, mk, mv)


def _ca_out_body(ca_ref, x1_ref, wco_ref, g1_ref, g2_ref, x2_ref, h3_ref):
    x2 = x1_ref[...] + _rms(_dot(ca_ref[...], wco_ref[...]), g1_ref[...])
    x2_ref[...] = x2
    h3_ref[...] = _rms(x2, g2_ref[...]).astype(bf16)


def _ca_out(ca, x1, wco_bf, g1, g2, tm):
    t = x1.shape[0]
    row = pl.BlockSpec((tm, D_MODEL), lambda i: (i, 0))
    return pl.pallas_call(
        _ca_out_body,
        grid=(t // tm,),
        in_specs=[row, row, _const_spec(wco_bf.shape), _const_spec((1, D_MODEL)), _const_spec((1, D_MODEL))],
        out_specs=[row, row],
        out_shape=[jax.ShapeDtypeStruct((t, D_MODEL), f32), jax.ShapeDtypeStruct((t, D_MODEL), bf16)],
        compiler_params=_cparams(("arbitrary",)),
        name="ca_out",
    )(ca, x1, wco_bf, g1, g2)


def _post_mix_body(x_ref, ohg_ref, osb_ref, wo_ref, g1_ref, g2_ref, wq_ref, mk_ref, mv_ref, wco_ref, g3_ref, g4_ref,
                   x2_ref, h3_ref):
    mixed = _dot(ohg_ref[...], wo_ref[0:HG_W, :]) + _dot(osb_ref[...], wo_ref[HG_W:HG_W + SB_W, :])
    x1 = x_ref[...] + _rms(mixed, g1_ref[...])
    qc = (_dot(_rms(x1, g2_ref[...]).astype(bf16), wq_ref[...]) * CA_SCALE).astype(bf16)
    mk = mk_ref[...].astype(bf16)
    mv = mv_ref[...].astype(bf16)
    ca = _cross_heads(qc, lambda h: mk[:, h * CA_DH:(h + 1) * CA_DH],
                      lambda h: mv[:, h * CA_DH:(h + 1) * CA_DH]).astype(bf16)
    x2 = x1 + _rms(_dot(ca, wco_ref[...]), g3_ref[...])
    x2_ref[...] = x2
    h3_ref[...] = _rms(x2, g4_ref[...]).astype(bf16)


def _post_mix(x, ohg, osb, wo_bf, g1, g2, wq_bf, mk, mv, wco_bf, g3, g4, n_seq, seq_len, tm):
    nblk = seq_len // tm
    row = lambda n: pl.BlockSpec((tm, n), lambda b, j: (b * nblk + j, 0))
    mem = pl.BlockSpec((N_MEM, D_MODEL), lambda b, j: (b, 0))
    gain = _const_spec((1, D_MODEL))
    return pl.pallas_call(
        _post_mix_body,
        grid=(n_seq, nblk),
        in_specs=[row(D_MODEL), row(HG_W), row(SB_W), _const_spec(wo_bf.shape), gain, gain,
                  _const_spec(wq_bf.shape), mem, mem, _const_spec(wco_bf.shape), gain, gain],
        out_specs=[row(D_MODEL), row(D_MODEL)],
        out_shape=[jax.ShapeDtypeStruct(x.shape, f32), jax.ShapeDtypeStruct(x.shape, bf16)],
        compiler_params=_cparams(("arbitrary", "arbitrary")),
        name="post_mix",
    )(x, ohg, osb, wo_bf, g1, g2, wq_bf, mk, mv, wco_bf, g3, g4)


FFN_CHUNK = 256


def _ffn_body(*refs, tm, dec, carry_tail):
    if carry_tail:
        h3_ref, x2_ref, wup_ref, cw_ref, cb_ref, wdn_ref, g_ref, y_ref, tail_ref, ext_scr, act_scr = refs
        halo_ref = None

        @pl.when(pl.program_id(1) == 0)
        def _():
            ext_scr[0:8, :] = jnp.zeros((8, ext_scr.shape[1]), f32)
    else:
        h3_ref, x2_ref, halo_ref, wup_ref, cw_ref, cb_ref, wdn_ref, g_ref, y_ref, u_ref, act_scr = refs
        pos = lax.broadcasted_iota(jnp.int32, (tm, 1), 0) % dec
    h3 = h3_ref[...]

    def conv_chunk(col):
        cs = slice(col, col + FFN_CHUNK)
        u = _dot(h3, wup_ref[:, cs])
        if carry_tail:
            ext_scr[8:8 + tm, cs] = u
            u1 = ext_scr[7:7 + tm, cs]
            u2 = ext_scr[6:6 + tm, cs]
            ext_scr[0:8, cs] = u[tm - 8:tm]
            tail_ref[:, cs] = u[tm - 8:tm]
        else:
            ext = jnp.concatenate([jnp.zeros((8, FFN_CHUNK), f32), u], axis=0)
            hal = halo_ref[:, cs]
            hext = jnp.concatenate([hal, jnp.zeros((8, FFN_CHUNK), f32)], axis=0)
            u1 = jnp.where(pos == 0, hext[1:1 + tm], ext[7:7 + tm])
            u2 = jnp.where(pos < 2, hal, ext[6:6 + tm])
            u_ref[:, cs] = u
        return cb_ref[:, cs] + cw_ref[2:3, cs] * u + cw_ref[1:2, cs] * u1 + cw_ref[0:1, cs] * u2

    for cj in range(D_FF // FFN_CHUNK):
        gate = conv_chunk(cj * FFN_CHUNK)
        val = conv_chunk(D_FF + cj * FFN_CHUNK)
        act_scr[:, cj * FFN_CHUNK:(cj + 1) * FFN_CHUNK] = (jax.nn.gelu(gate, approximate=True) * val).astype(bf16)
    y = _dot(act_scr[...], wdn_ref[...])
    y_ref[...] = x2_ref[...] + _rms(y, g_ref[...])


def _ffn_prompt(h3, x2, wup_bf, cw, cb, wdn_bf, g, n_seq, seq_len, tm):
    nblk = seq_len // tm
    row = pl.BlockSpec((tm, D_MODEL), lambda b, j: (b * nblk + j, 0))
    return pl.pallas_call(
        functools.partial(_ffn_body, tm=tm, dec=None, carry_tail=True),
        grid=(n_seq, nblk),
        in_specs=[row, row, _const_spec(wup_bf.shape), _const_spec(cw.shape), _const_spec(cb.shape),
                  _const_spec(wdn_bf.shape), _const_spec((1, D_MODEL))],
        out_specs=[row, pl.BlockSpec((8, 2 * D_FF), lambda b, j: (b, 0))],
        out_shape=[jax.ShapeDtypeStruct((n_seq * seq_len, D_MODEL), f32),
                   jax.ShapeDtypeStruct((n_seq * 8, 2 * D_FF), f32)],
        scratch_shapes=[pltpu.VMEM((tm + 8, 2 * D_FF), f32), pltpu.VMEM((tm, D_FF), bf16)],
        compiler_params=_cparams(("arbitrary", "arbitrary")),
        name="ffn_prompt",
    )(h3, x2, wup_bf, cw, cb, wdn_bf, g)


def _ffn_sample(h3, x2, halo, wup_bf, cw, cb, wdn_bf, g, dec, tm):
    t = h3.shape[0]
    row = pl.BlockSpec((tm, D_MODEL), lambda i: (i, 0))
    wide = pl.BlockSpec((tm, 2 * D_FF), lambda i: (i, 0))
    return pl.pallas_call(
        functools.partial(_ffn_body, tm=tm, dec=dec, carry_tail=False),
        grid=(t // tm,),
        in_specs=[row, row, wide, _const_spec(wup_bf.shape), _const_spec(cw.shape), _const_spec(cb.shape),
                  _const_spec(wdn_bf.shape), _const_spec((1, D_MODEL))],
        out_specs=[row, wide],
        out_shape=[jax.ShapeDtypeStruct((t, D_MODEL), f32), jax.ShapeDtypeStruct((t, 2 * D_FF), f32)],
        scratch_shapes=[pltpu.VMEM((tm, D_FF), bf16)],
        compiler_params=_cparams(("arbitrary",)),
        name="ffn_sample",
    )(h3, x2, halo, wup_bf, cw, cb, wdn_bf, g)


def kernel(x_prompt, x_sample, cache_sb_k, cache_sb_v, state_hgrn, state_ffn_conv, cache_mem_k, cache_mem_v,
           page_table, mem_prompt, w_in, hg_norm, hg_lb, sb_bias, w_o, g_mix_pre, g_mix_post, g_ca_pre, g_ca_post,
           g_mem, w_cq, w_ck, w_cv, w_co, g_ffn_pre, g_ffn_post, w_up, conv_w, conv_b, w_down):
    n_p, seq_len, _ = x_prompt.shape
    n_d, dec, _ = x_sample.shape
    depth = w_in.shape[0]
    assert depth == 1, "single-layer step"
    assert dec >= 2 and dec % 8 == 0, "the conv tail is taken from the new rows"
    l = 0
    row = lambda a: a[l].reshape(1, -1)
    w_in_bf, w_o_bf, w_cq_bf, w_co_bf = (w[l].astype(bf16) for w in (w_in, w_o, w_cq, w_co))
    w_ck_bf, w_cv_bf, w_up_bf, w_dn_bf = (w[l].astype(bf16) for w in (w_ck, w_cv, w_up, w_down))
    lbp = hg_lb[l:l + 2]
    gn = row(hg_norm)
    bias = sb_bias[l]
    cb = row(conv_b)
    cw = conv_w[l]

    xp = x_prompt.reshape(n_p * seq_len, D_MODEL)
    xs = x_sample.reshape(n_d * dec, D_MODEL)

    hin, sk_t, sv_t, qn, kb, vb = _in_proj(xp, row(g_mix_pre), w_in_bf, 512, n_seq=n_p)
    ohg, s_p = _hgrn_prompt(hin, lbp, gn, n_p, seq_len, 256, 16)
    osb = _sb_prompt(qn, kb, vb, bias, n_p, seq_len, 256)
    mk_p, mv_p = _mem_kv(mem_prompt.reshape(n_p * N_MEM, D_MODEL), row(g_mem), w_ck_bf, w_cv_bf, 256)
    x2, h3 = _post_mix(xp, ohg, osb, w_o_bf, row(g_mix_post), row(g_ca_pre), w_cq_bf, mk_p, mv_p, w_co_bf,
                       row(g_ca_post), row(g_ffn_pre), n_p, seq_len, 512)
    yp, tail_p = _ffn_prompt(h3, x2, w_up_bf, cw, cb, w_dn_bf, row(g_ffn_post), n_p, seq_len, 256)

    ts = n_d * dec
    hin_s, sk_s, sv_s, qn_s, kb_s, vb_s = _in_proj(xs, row(g_mix_pre), w_in_bf, 512)
    ohg_s, s_s = _hgrn_sample(hin_s, lbp, gn, state_hgrn[l], dec, 16)
    n_pool = cache_sb_k.shape[1]
    pages_t = lambda c: jnp.transpose(c[l], (0, 2, 3, 1)).reshape(n_pool, SB_W, PAGE)
    osb_s = _sb_sample(qn_s, kb_s, vb_s, pages_t(cache_sb_k), pages_t(cache_sb_v), page_table, bias,
                       dec).reshape(ts, SB_W)
    x1_s, qc_s = _mix_out(xs, ohg_s, osb_s, w_o_bf, row(g_mix_post), row(g_ca_pre), w_cq_bf, 512)
    ca_s = _cross_sample(qc_s, cache_mem_k[l], cache_mem_v[l], dec, 4)
    x2_s, h3_s = _ca_out(ca_s, x1_s, w_co_bf, row(g_ca_post), row(g_ffn_pre), 512)
    halo = jnp.pad(state_ffn_conv[l], ((0, 0), (0, dec - 2), (0, 0))).reshape(ts, 2 * D_FF)
    ys, u_s = _ffn_sample(h3_s, x2_s, halo, w_up_bf, cw, cb, w_dn_bf, row(g_ffn_post), dec, 128)

    kv_out = lambda a: jnp.transpose(a.reshape(n_p, SB_HEADS, SB_DH, seq_len), (0, 3, 1, 2))[None]
    return (yp.reshape(n_p, seq_len, D_MODEL), ys.reshape(n_d, dec, D_MODEL),
            kv_out(sk_t), kv_out(sv_t),
            s_p[None],
            tail_p.reshape(n_p, 8, 2 * D_FF)[None, :, 6:8],
            mk_p.reshape(1, n_p, N_MEM, CA_HEADS, CA_DH), mv_p.reshape(1, n_p, N_MEM, CA_HEADS, CA_DH),
            sk_s.reshape(1, n_d, dec, SB_HEADS, SB_DH), sv_s.reshape(1, n_d, dec, SB_HEADS, SB_DH),
            s_s[None],
            u_s.reshape(n_d, dec, 2 * D_FF)[None, :, dec - 2:dec])
```

```python
import functools

import numpy as np
import jax
import jax.numpy as jnp
from jax import lax
from jax.experimental import pallas as pl
from jax.experimental.pallas import tpu as pltpu

f32 = jnp.float32
bf16 = jnp.bfloat16

D_MODEL = 1024
HG_HEADS = 8
HG_DK = 64
HG_W = HG_HEADS * HG_DK
SB_HEADS = 8
SB_DH = 64
SB_W = SB_HEADS * SB_DH
SB_SCALE = SB_DH ** -0.5
LOG2E = 1.4426950408889634
N_MEM = 256
CA_HEADS = 4
CA_DH = D_MODEL // CA_HEADS
CA_SCALE = CA_DH ** -0.5
D_FF = 2816
RMS_EPS = 1e-6
PAGE = 128
LANES = 128
SUB = 8
VMEM_LIMIT = 56 * 1024 * 1024


def _cparams(sem, flags=None):
    return pltpu.CompilerParams(dimension_semantics=sem, vmem_limit_bytes=VMEM_LIMIT, flags=flags)


def _const_spec(shape):
    nd = len(shape)
    return pl.BlockSpec(shape, lambda *_: (0,) * nd, pipeline_mode=pl.Buffered(1))


def _rms(x, g):
    return x * lax.rsqrt(jnp.mean(x * x, axis=-1, keepdims=True) + RMS_EPS) * g


def _dot(a, b):
    return jnp.dot(a, b, preferred_element_type=f32)


def _dot_nt(a, b):
    return lax.dot_general(a, b, (((1,), (1,)), ((), ())), preferred_element_type=f32)


def _dot_tn(a, b):
    return lax.dot_general(a, b, (((0,), (0,)), ((), ())), preferred_element_type=f32)


def _in_proj_body(x_ref, g_ref, w_ref, hin_ref, sk_ref, sv_ref, qn_ref, kb_ref, vb_ref, *, kv_transposed):
    xn = _rms(x_ref[...], g_ref[...]).astype(bf16)
    for j in range(4):
        hin_ref[:, j * HG_W:(j + 1) * HG_W] = _dot(xn, w_ref[:, j * HG_W:(j + 1) * HG_W])
    base = 4 * HG_W
    q = _dot(xn, w_ref[:, base:base + SB_W])
    qn_ref[...] = (q * (-SB_SCALE * LOG2E)).astype(bf16)
    k = _dot(xn, w_ref[:, base + SB_W:base + 2 * SB_W])
    kb_ref[...] = k.astype(bf16)
    v = _dot(xn, w_ref[:, base + 2 * SB_W:base + 3 * SB_W])
    vb_ref[...] = v.astype(bf16)
    if kv_transposed:
        sk_ref[0] = k.T
        sv_ref[0] = v.T
    else:
        sk_ref[...] = k
        sv_ref[...] = v


def _in_proj(x, g, w_bf, tm, n_seq=None):
    t = x.shape[0]
    d_in = w_bf.shape[1]
    row = lambda n: pl.BlockSpec((tm, n), lambda i: (i, 0))
    if n_seq is None:
        kv_spec, kv_shape = row(SB_W), jax.ShapeDtypeStruct((t, SB_W), f32)
    else:
        nblk = t // n_seq // tm
        kv_spec = pl.BlockSpec((1, SB_W, tm), lambda i: (i // nblk, 0, i % nblk))
        kv_shape = jax.ShapeDtypeStruct((n_seq, SB_W, t // n_seq), f32)
    return pl.pallas_call(
        functools.partial(_in_proj_body, kv_transposed=n_seq is not None),
        grid=(t // tm,),
        in_specs=[row(D_MODEL), _const_spec((1, D_MODEL)), _const_spec((D_MODEL, d_in))],
        out_specs=[row(4 * HG_W), kv_spec, kv_spec, row(SB_W), row(SB_W), row(SB_W)],
        out_shape=[jax.ShapeDtypeStruct((t, 4 * HG_W), f32), kv_shape, kv_shape,
                   jax.ShapeDtypeStruct((t, SB_W), bf16), jax.ShapeDtypeStruct((t, SB_W), bf16),
                   jax.ShapeDtypeStruct((t, SB_W), bf16)],
        compiler_params=_cparams(("arbitrary",)),
        name="in_proj",
    )(x, g, w_bf)


def _hgrn_consts(nb, c):
    r = np.arange(nb)
    same = (r[:, None] // c) == (r[None, :] // c)
    tri = same & (r[None, :] <= r[:, None])
    stack = np.concatenate([tri, same], axis=0).astype(np.float32)
    h = np.arange(HG_W // 2) // HG_DK
    bo = (h[:, None] == h[None, :]).astype(np.float32)
    return jnp.asarray(stack, bf16), jnp.asarray(bo, bf16)


def _head_sums(x, bo):
    h = HG_W // 2
    return jnp.concatenate([_dot(x[:, :h], bo), _dot(x[:, h:], bo)], axis=1)


def _hgrn_prep(hin_ref, lbp_ref, stack_ref, scr, nb):
    q_scr, e_scr, v_scr, b_scr, qe_scr, kt_scr, el_scr = scr
    a = lbp_ref[...]
    e = jnp.exp(a - jnp.max(a, axis=0, keepdims=True))
    lb = e[0:1] / jnp.sum(e, axis=0, keepdims=True)
    hq = hin_ref[:, 0:HG_W]
    f = lb + (1.0 - lb) * jax.nn.sigmoid(hin_ref[:, HG_W:2 * HG_W])
    g = jnp.log2(f)
    kk = 1.0 - f
    g1 = g.astype(bf16)
    r1 = g - g1.astype(f32)
    g2 = r1.astype(bf16)
    g3 = (r1 - g2.astype(f32)).astype(bf16)
    st = stack_ref[...]
    bb = _dot(st, g1) + _dot(st, g2) + _dot(st, g3)
    b = bb[:nb]
    btot = bb[nb:]
    q_scr[...] = hq
    e_scr[...] = b - jnp.log2(kk)
    v_scr[...] = hin_ref[:, 2 * HG_W:3 * HG_W]
    b_scr[...] = b
    qe_scr[...] = hq * jnp.exp2(b)
    kt_scr[...] = kk * jnp.exp2(btot - b)
    el_scr[...] = jnp.exp2(btot)


def _hgrn_chunk(scr, bo, r0, c, st_list):
    q_scr, e_scr, v_scr, b_scr, qe_scr, kt_scr, el_scr = scr
    rows = pl.ds(r0, c)
    b_c = b_scr[rows, :]
    q_c = q_scr[rows, :]
    e_c = e_scr[rows, :]
    v_c = v_scr[rows, :]
    qe_c = qe_scr[rows, :].astype(bf16)
    kt_c = kt_scr[rows, :].astype(bf16)
    el_c = el_scr[pl.ds(r0, 1), :]
    v_cb = v_c.astype(bf16)
    n_sub = c // SUB
    t_loc = lax.broadcasted_iota(jnp.int32, (SUB, 1), 0)
    pieces, index = [], {}
    for s in range(c):
        for tg in range(s // SUB, n_sub):
            sl = slice(tg * SUB, (tg + 1) * SUB)
            pc = q_c[sl, :] * jnp.exp2(b_c[sl, :] - e_c[s:s + 1, :])
            if tg == s // SUB:
                pc = jnp.where(t_loc >= s % SUB, pc, 0.0)
            index[(s, tg)] = len(pieces)
            pieces.append(pc)
    p = jnp.concatenate(pieces, axis=0).astype(bf16)
    pw = _head_sums(p, bo)
    groups = []
    for tg in range(n_sub):
        acc = None
        for s in range((tg + 1) * SUB):
            k = index[(s, tg)]
            term = pw[k * SUB:(k + 1) * SUB, :] * v_c[s:s + 1, :]
            acc = term if acc is None else acc + term
        groups.append(acc)
    o = jnp.concatenate(groups, axis=0)
    ri = lax.broadcasted_iota(jnp.int32, (LANES, LANES), 0) // HG_DK
    ci = lax.broadcasted_iota(jnp.int32, (LANES, LANES), 1) // HG_DK
    same_head = ri == ci
    o_parts, new_states = [], []
    for pr in range(HG_W // LANES):
        sl = slice(pr * LANES, (pr + 1) * LANES)
        st = st_list[pr]
        o_parts.append(_dot_nt(qe_c[:, sl], st.astype(bf16)))
        upd = _dot_tn(v_cb[:, sl], kt_c[:, sl])
        new_states.append(st * el_c[:, sl] + jnp.where(same_head, upd, 0.0))
    return o + jnp.concatenate(o_parts, axis=1), new_states


def _hgrn_finish(o, hin_ref, gn_ref, bo, o_ref):
    ms = _head_sums((o * o).astype(bf16), bo) * (1.0 / HG_DK)
    gate = hin_ref[:, 3 * HG_W:4 * HG_W]
    o_ref[...] = (o * lax.rsqrt(ms + RMS_EPS) * gn_ref[...] * (gate * jax.nn.sigmoid(gate))).astype(o_ref.dtype)


def _pair_state_out(st):
    t = st.T
    return t[0:HG_DK, 0:HG_DK], t[HG_DK:LANES, HG_DK:LANES]


def _hgrn_prompt_body(hin_ref, lbp_ref, gn_ref, stack_ref, bo_ref, o_ref, sout_ref,
                      st_scr, oacc_scr, *scr, nb, c):
    j = pl.program_id(1)

    @pl.when(j == 0)
    def _():
        st_scr[...] = jnp.zeros_like(st_scr)

    _hgrn_prep(hin_ref, lbp_ref, stack_ref, scr, nb)
    bo = bo_ref[...]

    def step(ci, carry):
        r0 = pl.multiple_of(ci * c, c)
        o, new_states = _hgrn_chunk(scr, bo, r0, c, [st_scr[pr] for pr in range(HG_W // LANES)])
        for pr, st in enumerate(new_states):
            st_scr[pr] = st
        oacc_scr[pl.ds(r0, c), :] = o
        return carry

    lax.fori_loop(0, nb // c, step, 0, unroll=2)
    _hgrn_finish(oacc_scr[...], hin_ref, gn_ref, bo, o_ref)

    @pl.when(j == pl.num_programs(1) - 1)
    def _():
        for pr in range(HG_W // LANES):
            sa, sb = _pair_state_out(st_scr[pr])
            sout_ref[0, 2 * pr] = sa
            sout_ref[0, 2 * pr + 1] = sb


def _hgrn_prompt(hin, lbp, gn, n_seq, seq_len, nb, c):
    stack, bo = _hgrn_consts(nb, c)
    nblk = seq_len // nb
    scr = [pltpu.VMEM((nb, HG_W), f32) for _ in range(7)]
    return pl.pallas_call(
        functools.partial(_hgrn_prompt_body, nb=nb, c=c),
        grid=(n_seq, nblk),
        in_specs=[pl.BlockSpec((nb, 4 * HG_W), lambda b, j: (b * nblk + j, 0)),
                  _const_spec(lbp.shape), _const_spec((1, HG_W)),
                  _const_spec(stack.shape), _const_spec(bo.shape)],
        out_specs=[pl.BlockSpec((nb, HG_W), lambda b, j: (b * nblk + j, 0)),
                   pl.BlockSpec((1, HG_HEADS, HG_DK, HG_DK), lambda b, j: (b, 0, 0, 0))],
        out_shape=[jax.ShapeDtypeStruct((n_seq * seq_len, HG_W), bf16),
                   jax.ShapeDtypeStruct((n_seq, HG_HEADS, HG_DK, HG_DK), f32)],
        scratch_shapes=[pltpu.VMEM((HG_W // LANES, LANES, LANES), f32), pltpu.VMEM((nb, HG_W), f32)] + scr,
        compiler_params=_cparams(("arbitrary", "arbitrary")),
        name="hgrn_prompt",
    )(hin, lbp, gn, stack, bo)


def _hgrn_sample_body(hin_ref, lbp_ref, gn_ref, stack_ref, bo_ref, s0_ref, o_ref, sout_ref,
                      oacc_scr, *scr, nb, c):
    _hgrn_prep(hin_ref, lbp_ref, stack_ref, scr, nb)
    bo = bo_ref[...]
    zero = jnp.zeros((HG_DK, HG_DK), f32)

    def step(ci, carry):
        r0 = pl.multiple_of(ci * c, c)
        states = []
        for pr in range(HG_W // LANES):
            sa = s0_ref[ci, 2 * pr]
            sb = s0_ref[ci, 2 * pr + 1]
            bd = jnp.concatenate([jnp.concatenate([sa, zero], axis=1),
                                  jnp.concatenate([zero, sb], axis=1)], axis=0)
            states.append(bd.T)
        o, new_states = _hgrn_chunk(scr, bo, r0, c, states)
        for pr, st in enumerate(new_states):
            sa, sb = _pair_state_out(st)
            sout_ref[ci, 2 * pr] = sa
            sout_ref[ci, 2 * pr + 1] = sb
        oacc_scr[pl.ds(r0, c), :] = o
        return carry

    lax.fori_loop(0, nb // c, step, 0, unroll=2)
    _hgrn_finish(oacc_scr[...], hin_ref, gn_ref, bo, o_ref)


def _hgrn_sample(hin, lbp, gn, s0, c, seqs_per_step):
    n_seq = s0.shape[0]
    nb = seqs_per_step * c
    stack, bo = _hgrn_consts(nb, c)
    scr = [pltpu.VMEM((nb, HG_W), f32) for _ in range(7)]
    st_spec = pl.BlockSpec((seqs_per_step, HG_HEADS, HG_DK, HG_DK), lambda i: (i, 0, 0, 0))
    return pl.pallas_call(
        functools.partial(_hgrn_sample_body, nb=nb, c=c),
        grid=(n_seq // seqs_per_step,),
        in_specs=[pl.BlockSpec((nb, 4 * HG_W), lambda i: (i, 0)),
                  _const_spec(lbp.shape), _const_spec((1, HG_W)),
                  _const_spec(stack.shape), _const_spec(bo.shape), st_spec],
        out_specs=[pl.BlockSpec((nb, HG_W), lambda i: (i, 0)), st_spec],
        out_shape=[jax.ShapeDtypeStruct((n_seq * c, HG_W), bf16),
                   jax.ShapeDtypeStruct(s0.shape, f32)],
        scratch_shapes=[pltpu.VMEM((nb, HG_W), f32)] + scr,
        compiler_params=_cparams(("arbitrary",)),
        name="hgrn_sample",
    )(hin, lbp, gn, stack, bo, s0)


NEG_BIG = -1e30


def _sb_logs(zp, r, strict):
    c = jnp.minimum(zp, 0.0) - jnp.log2(1.0 + jnp.exp2(-jnp.abs(zp)))
    ls = (c - zp) + r
    if strict is not None:
        c = jnp.where(strict, c, 0.0)
        ls = jnp.where(strict, ls, NEG_BIG)
    return c.astype(bf16), ls, r + jnp.sum(c, axis=-1, keepdims=True)


def _sb_weights(ls, later):
    return jnp.exp2(ls + later).astype(bf16)


def _sb_prompt_body(bias_ref, q_ref, k_ref, v_ref, u_ref, o_ref,
                    kx_scr, qs_scr, zp_scr, c_scr, ls_scr, loc_scr, a_scr, acc_scr, r_scr, *, tq, tk):
    pr = pl.program_id(1)
    i = pl.program_id(2)
    nk = pl.num_programs(2) * (tq // tk)
    lane = lax.broadcasted_iota(jnp.int32, (1, LANES), 1)

    @pl.when(i == 0)
    def _():
        kx_scr[:, 0:LANES] = k_ref[...]
        kx_scr[:, LANES:2 * LANES] = jnp.broadcast_to(jnp.where(lane < 2, 1.0, 0.0), (kx_scr.shape[0], LANES)
                                                      ).astype(bf16)

    q = q_ref[...]
    zero_q = jnp.zeros_like(q)
    qs_scr[0:tq, 0:LANES] = jnp.where(lane < SB_DH, q, zero_q)
    qs_scr[tq:2 * tq, 0:LANES] = jnp.where(lane >= SB_DH, q, zero_q)
    row = lax.broadcasted_iota(jnp.int32, (2 * tq, 1), 0)
    nbias = jnp.where(row < tq, -LOG2E * bias_ref[2 * pr], -LOG2E * bias_ref[2 * pr + 1])
    nb_hi = nbias.astype(bf16).astype(f32)
    qs_scr[:, LANES:2 * LANES] = jnp.where(lane == 0, nb_hi, jnp.where(lane == 1, nbias - nb_hi, 0.0)).astype(bf16)
    t_row = lax.broadcasted_iota(jnp.int32, (2 * tq, tk), 0) % tq
    s_col = lax.broadcasted_iota(jnp.int32, (2 * tq, tk), 1)
    top = 2 * i + 1

    def kblock(ref, kb):
        return ref[pl.ds(pl.multiple_of(kb * tk, tk), tk), :]

    def scores(kb):
        return _dot_nt(qs_scr[...], kblock(kx_scr, jnp.maximum(kb, 0)))

    c0, ls0, r0 = _sb_logs(scores(top), jnp.zeros((2 * tq, 1), f32), s_col + tk < t_row)
    c_scr[0] = c0
    ls_scr[0] = ls0
    r_scr[...] = r0
    zp_scr[1] = scores(top - 1)
    a_scr[...] = jnp.zeros_like(a_scr)
    acc_scr[...] = jnp.zeros_like(acc_scr)

    def step(n, slot, strict):
        prev = 1 - slot
        acc_scr[...] += _dot(a_scr[...], kblock(v_ref, jnp.minimum(top - n + 2, nk - 1)))
        loc_scr[...] = _dot(c_scr[prev], u_ref[...])
        zp_scr[prev] = scores(top - n - 1)
        c, ls, r = _sb_logs(zp_scr[slot], r_scr[...], strict)
        c_scr[slot] = c
        ls_scr[slot] = ls
        r_scr[...] = r
        a_scr[...] = _sb_weights(ls_scr[prev], loc_scr[...])

    step(1, 1, s_col < t_row)

    def pair(j, carry):
        step(2 * j + 2, 0, None)
        step(2 * j + 3, 1, None)
        return carry

    lax.fori_loop(0, i, pair, 0)
    acc = acc_scr[...] + _dot(a_scr[...], kblock(v_ref, 1))
    a_last = _sb_weights(ls_scr[1], _dot(c_scr[1], u_ref[...]))
    acc = acc + _dot(a_last, kblock(v_ref, 0))
    o_ref[...] = jnp.where(lane < SB_DH, acc[:tq], acc[tq:]).astype(o_ref.dtype)


def _strict_upper(n):
    r = np.arange(n)
    return jnp.asarray((r[:, None] > r[None, :]).astype(np.float32), bf16)


def _sb_prompt(qn, kb, vb, bias, n_seq, seq_len, tk):
    tq = 2 * tk
    nq = seq_len // tq
    u = _strict_upper(tk)
    return pl.pallas_call(
        functools.partial(_sb_prompt_body, tq=tq, tk=tk),
        grid=(n_seq, SB_W // LANES, nq),
        in_specs=[pl.BlockSpec(memory_space=pltpu.SMEM),
                  pl.BlockSpec((tq, LANES), lambda b, p, i: (b * nq + i, p)),
                  pl.BlockSpec((seq_len, LANES), lambda b, p, i: (b, p)),
                  pl.BlockSpec((seq_len, LANES), lambda b, p, i: (b, p)),
                  _const_spec(u.shape)],
        out_specs=pl.BlockSpec((tq, LANES), lambda b, p, i: (b * nq + i, p)),
        out_shape=jax.ShapeDtypeStruct((n_seq * seq_len, SB_W), bf16),
        scratch_shapes=[pltpu.VMEM((seq_len, 2 * LANES), bf16), pltpu.VMEM((2 * tq, 2 * LANES), bf16),
                        pltpu.VMEM((2, 2 * tq, tk), f32), pltpu.VMEM((2, 2 * tq, tk), bf16),
                        pltpu.VMEM((2, 2 * tq, tk), f32), pltpu.VMEM((2 * tq, tk), f32),
                        pltpu.VMEM((2 * tq, tk), bf16), pltpu.VMEM((2 * tq, LANES), f32),
                        pltpu.VMEM((2 * tq, 1), f32)],
        compiler_params=_cparams(("arbitrary", "arbitrary", "arbitrary")),
        name="sb_prompt",
    )(bias, qn, kb, vb, u)


def _sb_sample_body(pt_ref, bias_ref, q_ref, kn_ref, vn_ref, u_ref, *rest, n_pages, dec):
    k_refs = rest[:n_pages]
    v_refs = rest[n_pages:2 * n_pages]
    o_ref = rest[2 * n_pages]
    rows = SB_HEADS * dec
    lane_head = lax.broadcasted_iota(jnp.int32, (1, SB_W), 1) // SB_DH
    q = q_ref[0].astype(f32)
    qbd = jnp.concatenate([jnp.where(lane_head == h, q, 0.0) for h in range(SB_HEADS)], axis=0).astype(bf16)
    row_head = lax.broadcasted_iota(jnp.int32, (rows, 1), 0) // dec
    nbias = jnp.zeros((rows, 1), f32)
    for h in range(SB_HEADS):
        nbias = jnp.where(row_head == h, -LOG2E * bias_ref[h], nbias)
    pad = jnp.zeros((PAGE - dec, SB_W), f32)
    kn = jnp.concatenate([kn_ref[0].astype(f32), pad], axis=0).astype(bf16)
    vn = jnp.concatenate([vn_ref[0].astype(f32), pad], axis=0).astype(bf16)
    t_row = lax.broadcasted_iota(jnp.int32, (rows, PAGE), 0) % dec
    s_col = lax.broadcasted_iota(jnp.int32, (rows, PAGE), 1)

    def pair(refs, g):
        return jnp.concatenate([refs[2 * g][0], refs[2 * g + 1][0]], axis=1).astype(bf16)

    groups = list(range(n_pages // 2 - 1, -1, -1))
    scores = [_dot_nt(qbd, kn)] + [_dot(qbd, pair(k_refs, g)) for g in groups]
    r = jnp.zeros((rows, 1), f32)
    cs, lss = [], []
    for n, zp in enumerate(scores):
        c, ls, r = _sb_logs(zp + nbias, r, (s_col < t_row) if n == 0 else None)
        cs.append(c)
        lss.append(ls)
    later = [_dot(cs[0], u_ref[0:PAGE, 0:PAGE])] + [_dot(c, u_ref[...]) for c in cs[1:]]
    acc = _dot(_sb_weights(lss[0], later[0]), vn)
    for n, g in enumerate(groups):
        acc = acc + _dot_nt(_sb_weights(lss[n + 1], later[n + 1]), pair(v_refs, g))
    out = jnp.zeros((dec, SB_W), f32)
    for h in range(SB_HEADS):
        out = out + jnp.where(lane_head == h, acc[h * dec:(h + 1) * dec, :], 0.0)
    o_ref[0] = out.astype(o_ref.dtype)


def _sb_sample(qn, kb, vb, cache_k, cache_v, page_table, bias, dec):
    n_seq, n_pages = page_table.shape
    assert n_pages % 2 == 0, "pages are consumed two at a time"
    u = _strict_upper(2 * PAGE)
    pt = page_table.reshape(-1)
    tok = pl.BlockSpec((1, dec, SB_W), lambda n, pt: (n, 0, 0))

    def page_spec(j):
        return pl.BlockSpec((1, SB_W, PAGE), lambda n, pt: (pt[n * n_pages + j], 0, 0))

    grid_spec = pltpu.PrefetchScalarGridSpec(
        num_scalar_prefetch=1,
        grid=(n_seq,),
        in_specs=[pl.BlockSpec(memory_space=pltpu.SMEM), tok, tok, tok,
                  pl.BlockSpec(u.shape, lambda n, pt: (0, 0))]
                 + [page_spec(j) for j in range(n_pages)] * 2,
        out_specs=tok,
    )
    return pl.pallas_call(
        functools.partial(_sb_sample_body, n_pages=n_pages, dec=dec),
        grid_spec=grid_spec,
        out_shape=jax.ShapeDtypeStruct((n_seq, dec, SB_W), bf16),
        compiler_params=_cparams(("arbitrary",)),
        name="sb_sample",
    )(pt, bias, qn.reshape(n_seq, dec, SB_W), kb.reshape(n_seq, dec, SB_W), vb.reshape(n_seq, dec, SB_W), u,
      *([cache_k] * n_pages), *([cache_v] * n_pages))


def _mix_out_body(x_ref, ohg_ref, osb_ref, wo_ref, g1_ref, g2_ref, wq_ref, x1_ref, qc_ref):
    mixed = _dot(ohg_ref[...], wo_ref[0:HG_W, :]) + _dot(osb_ref[...], wo_ref[HG_W:HG_W + SB_W, :])
    x1 = x_ref[...] + _rms(mixed, g1_ref[...])
    x1_ref[...] = x1
    h2 = _rms(x1, g2_ref[...]).astype(bf16)
    qc_ref[...] = (_dot(h2, wq_ref[...]) * CA_SCALE).astype(bf16)


def _mix_out(x, ohg, osb, wo_bf, g1, g2, wq_bf, tm):
    t = x.shape[0]
    row = lambda n: pl.BlockSpec((tm, n), lambda i: (i, 0))
    return pl.pallas_call(
        _mix_out_body,
        grid=(t // tm,),
        in_specs=[row(D_MODEL), row(HG_W), row(SB_W), _const_spec(wo_bf.shape),
                  _const_spec((1, D_MODEL)), _const_spec((1, D_MODEL)), _const_spec(wq_bf.shape)],
        out_specs=[row(D_MODEL), row(D_MODEL)],
        out_shape=[jax.ShapeDtypeStruct((t, D_MODEL), f32), jax.ShapeDtypeStruct((t, D_MODEL), bf16)],
        compiler_params=_cparams(("arbitrary",)),
        name="mix_out",
    )(x, ohg, osb, wo_bf, g1, g2, wq_bf)


def _mem_kv_body(m_ref, g_ref, wk_ref, wv_ref, mk_ref, mv_ref):
    mn = _rms(m_ref[...], g_ref[...]).astype(bf16)
    mk_ref[...] = _dot(mn, wk_ref[...])
    mv_ref[...] = _dot(mn, wv_ref[...])


def _mem_kv(mem, g, wk_bf, wv_bf, tm):
    t = mem.shape[0]
    row = pl.BlockSpec((tm, D_MODEL), lambda i: (i, 0))
    return pl.pallas_call(
        _mem_kv_body,
        grid=(t // tm,),
        in_specs=[row, _const_spec((1, D_MODEL)), _const_spec(wk_bf.shape), _const_spec(wv_bf.shape)],
        out_specs=[row, row],
        out_shape=[jax.ShapeDtypeStruct((t, D_MODEL), f32)] * 2,
        compiler_params=_cparams(("arbitrary",)),
        name="mem_kv",
    )(mem, g, wk_bf, wv_bf)


def _cross_heads(q, mk_head, mv_head):
    scores = [_dot_nt(q[:, h * CA_DH:(h + 1) * CA_DH], mk_head(h)) for h in range(CA_HEADS)]
    probs = []
    for s in scores:
        e = jnp.exp(s - jnp.max(s, axis=-1, keepdims=True))
        probs.append((e / jnp.sum(e, axis=-1, keepdims=True)).astype(bf16))
    return jnp.concatenate([_dot(p, mv_head(h)) for h, p in enumerate(probs)], axis=1)


def _cross_sample_body(q_ref, mk_hbm, mv_hbm, o_ref, kbuf, vbuf, sem, *, group, dec):
    i = pl.program_id(0)
    n_steps = pl.num_programs(0)

    def copies(step, slot):
        out = []
        for s in range(group):
            for h in range(CA_HEADS):
                n = step * group + s
                out.append(pltpu.make_async_copy(mk_hbm.at[n, :, h, :], kbuf.at[slot, s, h], sem.at[slot]))
                out.append(pltpu.make_async_copy(mv_hbm.at[n, :, h, :], vbuf.at[slot, s, h], sem.at[slot]))
        return out

    slot = i % 2

    @pl.when(i == 0)
    def _():
        for cp in copies(0, 0):
            cp.start()

    @pl.when(i + 1 < n_steps)
    def _():
        for cp in copies(i + 1, 1 - slot):
            cp.start()

    for cp in copies(i, slot):
        cp.wait()
    scores = [_dot_nt(q_ref[s * dec:(s + 1) * dec, h * CA_DH:(h + 1) * CA_DH], kbuf[slot, s, h].astype(bf16))
              for s in range(group) for h in range(CA_HEADS)]
    sc = jnp.concatenate(scores, axis=0)
    e = jnp.exp(sc - jnp.max(sc, axis=-1, keepdims=True))
    p = e / jnp.sum(e, axis=-1, keepdims=True)
    for s in range(group):
        outs = [_dot(p[(s * CA_HEADS + h) * dec:(s * CA_HEADS + h + 1) * dec, :].astype(bf16),
                     vbuf[slot, s, h].astype(bf16)) for h in range(CA_HEADS)]
        o_ref[s * dec:(s + 1) * dec, :] = jnp.concatenate(outs, axis=1).astype(o_ref.dtype)


def _cross_sample(qc, mk, mv, dec, group):
    n_seq = mk.shape[0]
    row = pl.BlockSpec((group * dec, D_MODEL), lambda i: (i, 0))
    buf = pltpu.VMEM((2, group, CA_HEADS, N_MEM, CA_DH), f32)
    return pl.pallas_call(
        functools.partial(_cross_sample_body, group=group, dec=dec),
        grid=(n_seq // group,),
        in_specs=[row, pl.BlockSpec(memory_space=pl.ANY), pl.BlockSpec(memory_space=pl.ANY)],
        out_specs=row,
        out_shape=jax.ShapeDtypeStruct(qc.shape, bf16),
        scratch_shapes=[buf, buf, pltpu.SemaphoreType.DMA((2,))],
        compiler_params=_cparams(("arbitrary",)),
        name="cross_sample",
    )(qc, mk, mv)


def _ca_out_body(ca_ref, x1_ref, wco_ref, g1_ref, g2_ref, x2_ref, h3_ref):
    x2 = x1_ref[...] + _rms(_dot(ca_ref[...], wco_ref[...]), g1_ref[...])
    x2_ref[...] = x2
    h3_ref[...] = _rms(x2, g2_ref[...]).astype(bf16)


def _ca_out(ca, x1, wco_bf, g1, g2, tm):
    t = x1.shape[0]
    row = pl.BlockSpec((tm, D_MODEL), lambda i: (i, 0))
    return pl.pallas_call(
        _ca_out_body,
        grid=(t // tm,),
        in_specs=[row, row, _const_spec(wco_bf.shape), _const_spec((1, D_MODEL)), _const_spec((1, D_MODEL))],
        out_specs=[row, row],
        out_shape=[jax.ShapeDtypeStruct((t, D_MODEL), f32), jax.ShapeDtypeStruct((t, D_MODEL), bf16)],
        compiler_params=_cparams(("arbitrary",)),
        name="ca_out",
    )(ca, x1, wco_bf, g1, g2)


FFN_CHUNK = 256


def _ffn_core(h3, x2, wup_ref, cw_ref, cb_ref, wdn_ref, g_ref, y_ref, act_scr, tm, *, tail=None, halo=None):
    if halo is not None:
        pos = lax.broadcasted_iota(jnp.int32, (tm, 1), 0) % halo[2]

    def conv_chunk(col):
        cs = slice(col, col + FFN_CHUNK)
        u = _dot(h3, wup_ref[:, cs])
        if tail is not None:
            tail_ref, ext_scr = tail
            ext_scr[8:8 + tm, cs] = u
            u1 = ext_scr[7:7 + tm, cs]
            u2 = ext_scr[6:6 + tm, cs]
            ext_scr[0:8, cs] = u[tm - 8:tm]
            tail_ref[:, cs] = u[tm - 8:tm]
        else:
            halo_ref, u_ref, _ = halo
            ext = jnp.concatenate([jnp.zeros((8, FFN_CHUNK), f32), u], axis=0)
            hal = halo_ref[:, cs]
            hext = jnp.concatenate([hal, jnp.zeros((8, FFN_CHUNK), f32)], axis=0)
            u1 = jnp.where(pos == 0, hext[1:1 + tm], ext[7:7 + tm])
            u2 = jnp.where(pos < 2, hal, ext[6:6 + tm])
            u_ref[:, cs] = u
        return cb_ref[:, cs] + cw_ref[2:3, cs] * u + cw_ref[1:2, cs] * u1 + cw_ref[0:1, cs] * u2

    for cj in range(D_FF // FFN_CHUNK):
        gate = conv_chunk(cj * FFN_CHUNK)
        val = conv_chunk(D_FF + cj * FFN_CHUNK)
        act_scr[:, cj * FFN_CHUNK:(cj + 1) * FFN_CHUNK] = (jax.nn.gelu(gate, approximate=True) * val).astype(bf16)
    y = _dot(act_scr[...], wdn_ref[...])
    y_ref[...] = x2 + _rms(y, g_ref[...])


def _ffn_sample_body(h3_ref, x2_ref, halo_ref, wup_ref, cw_ref, cb_ref, wdn_ref, g_ref, y_ref, u_ref, act_scr,
                     *, tm, dec):
    _ffn_core(h3_ref[...], x2_ref[...], wup_ref, cw_ref, cb_ref, wdn_ref, g_ref, y_ref, act_scr, tm,
              halo=(halo_ref, u_ref, dec))


def _post_ffn_body(x_ref, ohg_ref, osb_ref, wo_ref, g1_ref, g2_ref, wq_ref, mk_ref, mv_ref, wco_ref, g3_ref, g4_ref,
                   wup_ref, cw_ref, cb_ref, wdn_ref, g5_ref, y_ref, tail_ref, ext_scr, act_scr, *, tm):
    @pl.when(pl.program_id(1) == 0)
    def _():
        ext_scr[0:8, :] = jnp.zeros((8, ext_scr.shape[1]), f32)

    mixed = _dot(ohg_ref[...], wo_ref[0:HG_W, :]) + _dot(osb_ref[...], wo_ref[HG_W:HG_W + SB_W, :])
    x1 = x_ref[...] + _rms(mixed, g1_ref[...])
    qc = (_dot(_rms(x1, g2_ref[...]).astype(bf16), wq_ref[...]) * CA_SCALE).astype(bf16)
    mk = mk_ref[...].astype(bf16)
    mv = mv_ref[...].astype(bf16)
    ca = _cross_heads(qc, lambda h: mk[:, h * CA_DH:(h + 1) * CA_DH],
                      lambda h: mv[:, h * CA_DH:(h + 1) * CA_DH]).astype(bf16)
    x2 = x1 + _rms(_dot(ca, wco_ref[...]), g3_ref[...])
    h3 = _rms(x2, g4_ref[...]).astype(bf16)
    _ffn_core(h3, x2, wup_ref, cw_ref, cb_ref, wdn_ref, g5_ref, y_ref, act_scr, tm, tail=(tail_ref, ext_scr))


def _post_ffn(x, ohg, osb, wo_bf, g1, g2, wq_bf, mk, mv, wco_bf, g3, g4, wup_bf, cw, cb, wdn_bf, g5,
              n_seq, seq_len, tm):
    nblk = seq_len // tm
    row = lambda n: pl.BlockSpec((tm, n), lambda b, j: (b * nblk + j, 0))
    mem = pl.BlockSpec((N_MEM, D_MODEL), lambda b, j: (b, 0))
    gain = _const_spec((1, D_MODEL))
    return pl.pallas_call(
        functools.partial(_post_ffn_body, tm=tm),
        grid=(n_seq, nblk),
        in_specs=[row(D_MODEL), row(HG_W), row(SB_W), _const_spec(wo_bf.shape), gain, gain,
                  _const_spec(wq_bf.shape), mem, mem, _const_spec(wco_bf.shape), gain, gain,
                  _const_spec(wup_bf.shape), _const_spec(cw.shape), _const_spec(cb.shape),
                  _const_spec(wdn_bf.shape), gain],
        out_specs=[row(D_MODEL), pl.BlockSpec((8, 2 * D_FF), lambda b, j: (b, 0))],
        out_shape=[jax.ShapeDtypeStruct(x.shape, f32), jax.ShapeDtypeStruct((n_seq * 8, 2 * D_FF), f32)],
        scratch_shapes=[pltpu.VMEM((tm + 8, 2 * D_FF), f32), pltpu.VMEM((tm, D_FF), bf16)],
        compiler_params=_cparams(("arbitrary", "arbitrary")),
        name="post_ffn",
    )(x, ohg, osb, wo_bf, g1, g2, wq_bf, mk, mv, wco_bf, g3, g4, wup_bf, cw, cb, wdn_bf, g5)


def _ffn_sample(h3, x2, halo, wup_bf, cw, cb, wdn_bf, g, dec, tm):
    t = h3.shape[0]
    row = pl.BlockSpec((tm, D_MODEL), lambda i: (i, 0))
    wide = pl.BlockSpec((tm, 2 * D_FF), lambda i: (i, 0))
    return pl.pallas_call(
        functools.partial(_ffn_sample_body, tm=tm, dec=dec),
        grid=(t // tm,),
        in_specs=[row, row, wide, _const_spec(wup_bf.shape), _const_spec(cw.shape), _const_spec(cb.shape),
                  _const_spec(wdn_bf.shape), _const_spec((1, D_MODEL))],
        out_specs=[row, wide],
        out_shape=[jax.ShapeDtypeStruct((t, D_MODEL), f32), jax.ShapeDtypeStruct((t, 2 * D_FF), f32)],
        scratch_shapes=[pltpu.VMEM((tm, D_FF), bf16)],
        compiler_params=_cparams(("arbitrary",)),
        name="ffn_sample",
    )(h3, x2, halo, wup_bf, cw, cb, wdn_bf, g)


def kernel(x_prompt, x_sample, cache_sb_k, cache_sb_v, state_hgrn, state_ffn_conv, cache_mem_k, cache_mem_v,
           page_table, mem_prompt, w_in, hg_norm, hg_lb, sb_bias, w_o, g_mix_pre, g_mix_post, g_ca_pre, g_ca_post,
           g_mem, w_cq, w_ck, w_cv, w_co, g_ffn_pre, g_ffn_post, w_up, conv_w, conv_b, w_down):
    n_p, seq_len, _ = x_prompt.shape
    n_d, dec, _ = x_sample.shape
    depth = w_in.shape[0]
    assert depth == 1, "single-layer step"
    assert dec >= 2 and dec % 8 == 0, "the conv tail is taken from the new rows"
    l = 0
    row = lambda a: a[l].reshape(1, -1)
    w_in_bf, w_o_bf, w_cq_bf, w_co_bf = (w[l].astype(bf16) for w in (w_in, w_o, w_cq, w_co))
    w_ck_bf, w_cv_bf, w_up_bf, w_dn_bf = (w[l].astype(bf16) for w in (w_ck, w_cv, w_up, w_down))
    lbp = hg_lb[l:l + 2]
    gn = row(hg_norm)
    bias = sb_bias[l]
    cb = row(conv_b)
    cw = conv_w[l]

    xp = x_prompt.reshape(n_p * seq_len, D_MODEL)
    xs = x_sample.reshape(n_d * dec, D_MODEL)

    hin, sk_t, sv_t, qn, kb, vb = _in_proj(xp, row(g_mix_pre), w_in_bf, 512, n_seq=n_p)
    ohg, s_p = _hgrn_prompt(hin, lbp, gn, n_p, seq_len, 256, 16)
    osb = _sb_prompt(qn, kb, vb, bias, n_p, seq_len, 256)
    mk_p, mv_p = _mem_kv(mem_prompt.reshape(n_p * N_MEM, D_MODEL), row(g_mem), w_ck_bf, w_cv_bf, 256)
    yp, tail_p = _post_ffn(xp, ohg, osb, w_o_bf, row(g_mix_post), row(g_ca_pre), w_cq_bf, mk_p, mv_p, w_co_bf,
                           row(g_ca_post), row(g_ffn_pre), w_up_bf, cw, cb, w_dn_bf, row(g_ffn_post),
                           n_p, seq_len, 256)

    ts = n_d * dec
    hin_s, sk_s, sv_s, qn_s, kb_s, vb_s = _in_proj(xs, row(g_mix_pre), w_in_bf, 512)
    ohg_s, s_s = _hgrn_sample(hin_s, lbp, gn, state_hgrn[l], dec, 16)
    n_pool = cache_sb_k.shape[1]
    pages_t = lambda c: jnp.transpose(c[l], (0, 2, 3, 1)).reshape(n_pool, SB_W, PAGE)
    osb_s = _sb_sample(qn_s, kb_s, vb_s, pages_t(cache_sb_k), pages_t(cache_sb_v), page_table, bias,
                       dec).reshape(ts, SB_W)
    x1_s, qc_s = _mix_out(xs, ohg_s, osb_s, w_o_bf, row(g_mix_post), row(g_ca_pre), w_cq_bf, 512)
    ca_s = _cross_sample(qc_s, cache_mem_k[l], cache_mem_v[l], dec, 4)
    x2_s, h3_s = _ca_out(ca_s, x1_s, w_co_bf, row(g_ca_post), row(g_ffn_pre), 512)
    halo = jnp.pad(state_ffn_conv[l], ((0, 0), (0, dec - 2), (0, 0))).reshape(ts, 2 * D_FF)
    ys, u_s = _ffn_sample(h3_s, x2_s, halo, w_up_bf, cw, cb, w_dn_bf, row(g_ffn_post), dec, 128)

    kv_out = lambda a: jnp.transpose(a.reshape(n_p, SB_HEADS, SB_DH, seq_len), (0, 3, 1, 2))[None]
    return (yp.reshape(n_p, seq_len, D_MODEL), ys.reshape(n_d, dec, D_MODEL),
            kv_out(sk_t), kv_out(sv_t),
            s_p[None],
            tail_p.reshape(n_p, 8, 2 * D_FF)[None, :, 6:8],
            mk_p.reshape(1, n_p, N_MEM, CA_HEADS, CA_DH), mv_p.reshape(1, n_p, N_MEM, CA_HEADS, CA_DH),
            sk_s.reshape(1, n_d, dec, SB_HEADS, SB_DH), sv_s.reshape(1, n_d, dec, SB_HEADS, SB_DH),
            s_s[None],
            u_s.reshape(n_d, dec, 2 * D_FF)[None, :, dec - 2:dec])
```

```python
import functools

import numpy as np
import jax
import jax.numpy as jnp
from jax import lax
from jax.experimental import pallas as pl
from jax.experimental.pallas import tpu as pltpu

f32 = jnp.float32
bf16 = jnp.bfloat16

D_MODEL = 1024
HG_HEADS = 8
HG_DK = 64
HG_W = HG_HEADS * HG_DK
SB_HEADS = 8
SB_DH = 64
SB_W = SB_HEADS * SB_DH
SB_SCALE = SB_DH ** -0.5
LOG2E = 1.4426950408889634
N_MEM = 256
CA_HEADS = 4
CA_DH = D_MODEL // CA_HEADS
CA_SCALE = CA_DH ** -0.5
D_FF = 2816
RMS_EPS = 1e-6
PAGE = 128
LANES = 128
SUB = 8
VMEM_LIMIT = 56 * 1024 * 1024


def _cparams(sem, flags=None):
    return pltpu.CompilerParams(dimension_semantics=sem, vmem_limit_bytes=VMEM_LIMIT, flags=flags)


def _const_spec(shape):
    nd = len(shape)
    return pl.BlockSpec(shape, lambda *_: (0,) * nd, pipeline_mode=pl.Buffered(1))


def _rms(x, g):
    return x * lax.rsqrt(jnp.mean(x * x, axis=-1, keepdims=True) + RMS_EPS) * g


def _dot(a, b):
    return jnp.dot(a, b, preferred_element_type=f32)


def _dot_nt(a, b):
    return lax.dot_general(a, b, (((1,), (1,)), ((), ())), preferred_element_type=f32)


def _dot_tn(a, b):
    return lax.dot_general(a, b, (((0,), (0,)), ((), ())), preferred_element_type=f32)


def _in_proj_body(x_ref, g_ref, w_ref, hin_ref, sk_ref, sv_ref, qn_ref, kb_ref, vb_ref, *, kv_transposed):
    xn = _rms(x_ref[...], g_ref[...]).astype(bf16)
    for j in range(4):
        hin_ref[:, j * HG_W:(j + 1) * HG_W] = _dot(xn, w_ref[:, j * HG_W:(j + 1) * HG_W])
    base = 4 * HG_W
    q = _dot(xn, w_ref[:, base:base + SB_W])
    qn_ref[...] = (q * (-SB_SCALE * LOG2E)).astype(bf16)
    k = _dot(xn, w_ref[:, base + SB_W:base + 2 * SB_W])
    kb_ref[...] = k.astype(bf16)
    v = _dot(xn, w_ref[:, base + 2 * SB_W:base + 3 * SB_W])
    vb_ref[...] = v.astype(bf16)
    if kv_transposed:
        sk_ref[0] = k.T
        sv_ref[0] = v.T
    else:
        sk_ref[...] = k
        sv_ref[...] = v


def _in_proj(x, g, w_bf, tm, n_seq=None):
    t = x.shape[0]
    d_in = w_bf.shape[1]
    row = lambda n: pl.BlockSpec((tm, n), lambda i: (i, 0))
    if n_seq is None:
        kv_spec, kv_shape = row(SB_W), jax.ShapeDtypeStruct((t, SB_W), f32)
    else:
        nblk = t // n_seq // tm
        kv_spec = pl.BlockSpec((1, SB_W, tm), lambda i: (i // nblk, 0, i % nblk))
        kv_shape = jax.ShapeDtypeStruct((n_seq, SB_W, t // n_seq), f32)
    return pl.pallas_call(
        functools.partial(_in_proj_body, kv_transposed=n_seq is not None),
        grid=(t // tm,),
        in_specs=[row(D_MODEL), _const_spec((1, D_MODEL)), _const_spec((D_MODEL, d_in))],
        out_specs=[row(4 * HG_W), kv_spec, kv_spec, row(SB_W), row(SB_W), row(SB_W)],
        out_shape=[jax.ShapeDtypeStruct((t, 4 * HG_W), f32), kv_shape, kv_shape,
                   jax.ShapeDtypeStruct((t, SB_W), bf16), jax.ShapeDtypeStruct((t, SB_W), bf16),
                   jax.ShapeDtypeStruct((t, SB_W), bf16)],
        compiler_params=_cparams(("arbitrary",)),
        name="in_proj",
    )(x, g, w_bf)


def _hgrn_consts(nb, c):
    r = np.arange(nb)
    same = (r[:, None] // c) == (r[None, :] // c)
    tri = same & (r[None, :] <= r[:, None])
    stack = np.concatenate([tri, same], axis=0).astype(np.float32)
    h = np.arange(HG_W // 2) // HG_DK
    bo = (h[:, None] == h[None, :]).astype(np.float32)
    return jnp.asarray(stack, bf16), jnp.asarray(bo, bf16)


def _head_sums(x, bo):
    h = HG_W // 2
    return jnp.concatenate([_dot(x[:, :h], bo), _dot(x[:, h:], bo)], axis=1)


def _hgrn_prep(hin_ref, lbp_ref, stack_ref, scr, nb):
    q_scr, e_scr, v_scr, b_scr, qe_scr, kt_scr, el_scr = scr
    a = lbp_ref[...]
    e = jnp.exp(a - jnp.max(a, axis=0, keepdims=True))
    lb = e[0:1] / jnp.sum(e, axis=0, keepdims=True)
    hq = hin_ref[:, 0:HG_W]
    f = lb + (1.0 - lb) * jax.nn.sigmoid(hin_ref[:, HG_W:2 * HG_W])
    g = jnp.log2(f)
    kk = 1.0 - f
    g1 = g.astype(bf16)
    r1 = g - g1.astype(f32)
    g2 = r1.astype(bf16)
    g3 = (r1 - g2.astype(f32)).astype(bf16)
    st = stack_ref[...]
    bb = _dot(st, g1) + _dot(st, g2) + _dot(st, g3)
    b = bb[:nb]
    btot = bb[nb:]
    q_scr[...] = hq
    e_scr[...] = b - jnp.log2(kk)
    v_scr[...] = hin_ref[:, 2 * HG_W:3 * HG_W]
    b_scr[...] = b
    qe_scr[...] = hq * jnp.exp2(b)
    kt_scr[...] = kk * jnp.exp2(btot - b)
    el_scr[...] = jnp.exp2(btot)


def _hgrn_chunk(scr, bo, r0, c, st_list):
    q_scr, e_scr, v_scr, b_scr, qe_scr, kt_scr, el_scr = scr
    rows = pl.ds(r0, c)
    b_c = b_scr[rows, :]
    q_c = q_scr[rows, :]
    e_c = e_scr[rows, :]
    v_c = v_scr[rows, :]
    qe_c = qe_scr[rows, :].astype(bf16)
    kt_c = kt_scr[rows, :].astype(bf16)
    el_c = el_scr[pl.ds(r0, 1), :]
    v_cb = v_c.astype(bf16)
    n_sub = c // SUB
    t_loc = lax.broadcasted_iota(jnp.int32, (SUB, 1), 0)
    pieces, index = [], {}
    for s in range(c):
        for tg in range(s // SUB, n_sub):
            sl = slice(tg * SUB, (tg + 1) * SUB)
            pc = q_c[sl, :] * jnp.exp2(b_c[sl, :] - e_c[s:s + 1, :])
            if tg == s // SUB:
                pc = jnp.where(t_loc >= s % SUB, pc, 0.0)
            index[(s, tg)] = len(pieces)
            pieces.append(pc)
    p = jnp.concatenate(pieces, axis=0).astype(bf16)
    pw = _head_sums(p, bo)
    groups = []
    for tg in range(n_sub):
        acc = None
        for s in range((tg + 1) * SUB):
            k = index[(s, tg)]
            term = pw[k * SUB:(k + 1) * SUB, :] * v_c[s:s + 1, :]
            acc = term if acc is None else acc + term
        groups.append(acc)
    o = jnp.concatenate(groups, axis=0)
    ri = lax.broadcasted_iota(jnp.int32, (LANES, LANES), 0) // HG_DK
    ci = lax.broadcasted_iota(jnp.int32, (LANES, LANES), 1) // HG_DK
    same_head = ri == ci
    o_parts, new_states = [], []
    for pr in range(HG_W // LANES):
        sl = slice(pr * LANES, (pr + 1) * LANES)
        st = st_list[pr]
        o_parts.append(_dot_nt(qe_c[:, sl], st.astype(bf16)))
        upd = _dot_tn(v_cb[:, sl], kt_c[:, sl])
        new_states.append(st * el_c[:, sl] + jnp.where(same_head, upd, 0.0))
    return o + jnp.concatenate(o_parts, axis=1), new_states


def _hgrn_finish(o, hin_ref, gn_ref, bo, o_ref):
    ms = _head_sums((o * o).astype(bf16), bo) * (1.0 / HG_DK)
    gate = hin_ref[:, 3 * HG_W:4 * HG_W]
    o_ref[...] = (o * lax.rsqrt(ms + RMS_EPS) * gn_ref[...] * (gate * jax.nn.sigmoid(gate))).astype(o_ref.dtype)


def _pair_state_out(st):
    t = st.T
    return t[0:HG_DK, 0:HG_DK], t[HG_DK:LANES, HG_DK:LANES]


def _hgrn_prompt_body(hin_ref, lbp_ref, gn_ref, stack_ref, bo_ref, o_ref, sout_ref,
                      st_scr, oacc_scr, *scr, nb, c):
    j = pl.program_id(1)

    @pl.when(j == 0)
    def _():
        st_scr[...] = jnp.zeros_like(st_scr)

    _hgrn_prep(hin_ref, lbp_ref, stack_ref, scr, nb)
    bo = bo_ref[...]

    def step(ci, carry):
        r0 = pl.multiple_of(ci * c, c)
        o, new_states = _hgrn_chunk(scr, bo, r0, c, [st_scr[pr] for pr in range(HG_W // LANES)])
        for pr, st in enumerate(new_states):
            st_scr[pr] = st
        oacc_scr[pl.ds(r0, c), :] = o
        return carry

    lax.fori_loop(0, nb // c, step, 0, unroll=2)
    _hgrn_finish(oacc_scr[...], hin_ref, gn_ref, bo, o_ref)

    @pl.when(j == pl.num_programs(1) - 1)
    def _():
        for pr in range(HG_W // LANES):
            sa, sb = _pair_state_out(st_scr[pr])
            sout_ref[0, 2 * pr] = sa
            sout_ref[0, 2 * pr + 1] = sb


def _hgrn_prompt(hin, lbp, gn, n_seq, seq_len, nb, c):
    stack, bo = _hgrn_consts(nb, c)
    nblk = seq_len // nb
    scr = [pltpu.VMEM((nb, HG_W), f32) for _ in range(7)]
    return pl.pallas_call(
        functools.partial(_hgrn_prompt_body, nb=nb, c=c),
        grid=(n_seq, nblk),
        in_specs=[pl.BlockSpec((nb, 4 * HG_W), lambda b, j: (b * nblk + j, 0)),
                  _const_spec(lbp.shape), _const_spec((1, HG_W)),
                  _const_spec(stack.shape), _const_spec(bo.shape)],
        out_specs=[pl.BlockSpec((nb, HG_W), lambda b, j: (b * nblk + j, 0)),
                   pl.BlockSpec((1, HG_HEADS, HG_DK, HG_DK), lambda b, j: (b, 0, 0, 0))],
        out_shape=[jax.ShapeDtypeStruct((n_seq * seq_len, HG_W), bf16),
                   jax.ShapeDtypeStruct((n_seq, HG_HEADS, HG_DK, HG_DK), f32)],
        scratch_shapes=[pltpu.VMEM((HG_W // LANES, LANES, LANES), f32), pltpu.VMEM((nb, HG_W), f32)] + scr,
        compiler_params=_cparams(("arbitrary", "arbitrary")),
        name="hgrn_prompt",
    )(hin, lbp, gn, stack, bo)


def _hgrn_sample_body(hin_ref, lbp_ref, gn_ref, stack_ref, bo_ref, s0_ref, o_ref, sout_ref,
                      oacc_scr, *scr, nb, c):
    _hgrn_prep(hin_ref, lbp_ref, stack_ref, scr, nb)
    bo = bo_ref[...]
    zero = jnp.zeros((HG_DK, HG_DK), f32)

    def step(ci, carry):
        r0 = pl.multiple_of(ci * c, c)
        states = []
        for pr in range(HG_W // LANES):
            sa = s0_ref[ci, 2 * pr]
            sb = s0_ref[ci, 2 * pr + 1]
            bd = jnp.concatenate([jnp.concatenate([sa, zero], axis=1),
                                  jnp.concatenate([zero, sb], axis=1)], axis=0)
            states.append(bd.T)
        o, new_states = _hgrn_chunk(scr, bo, r0, c, states)
        for pr, st in enumerate(new_states):
            sa, sb = _pair_state_out(st)
            sout_ref[ci, 2 * pr] = sa
            sout_ref[ci, 2 * pr + 1] = sb
        oacc_scr[pl.ds(r0, c), :] = o
        return carry

    lax.fori_loop(0, nb // c, step, 0, unroll=2)
    _hgrn_finish(oacc_scr[...], hin_ref, gn_ref, bo, o_ref)


def _hgrn_sample(hin, lbp, gn, s0, c, seqs_per_step):
    n_seq = s0.shape[0]
    nb = seqs_per_step * c
    stack, bo = _hgrn_consts(nb, c)
    scr = [pltpu.VMEM((nb, HG_W), f32) for _ in range(7)]
    st_spec = pl.BlockSpec((seqs_per_step, HG_HEADS, HG_DK, HG_DK), lambda i: (i, 0, 0, 0))
    return pl.pallas_call(
        functools.partial(_hgrn_sample_body, nb=nb, c=c),
        grid=(n_seq // seqs_per_step,),
        in_specs=[pl.BlockSpec((nb, 4 * HG_W), lambda i: (i, 0)),
                  _const_spec(lbp.shape), _const_spec((1, HG_W)),
                  _const_spec(stack.shape), _const_spec(bo.shape), st_spec],
        out_specs=[pl.BlockSpec((nb, HG_W), lambda i: (i, 0)), st_spec],
        out_shape=[jax.ShapeDtypeStruct((n_seq * c, HG_W), bf16),
                   jax.ShapeDtypeStruct(s0.shape, f32)],
        scratch_shapes=[pltpu.VMEM((nb, HG_W), f32)] + scr,
        compiler_params=_cparams(("arbitrary",)),
        name="hgrn_sample",
    )(hin, lbp, gn, stack, bo, s0)


NEG_BIG = -1e30


def _sb_logs(zp, r, strict):
    c = jnp.minimum(zp, 0.0) - jnp.log2(1.0 + jnp.exp2(-jnp.abs(zp)))
    ls = (c - zp) + r
    if strict is not None:
        c = jnp.where(strict, c, 0.0)
        ls = jnp.where(strict, ls, NEG_BIG)
    return c.astype(bf16), ls, r + jnp.sum(c, axis=-1, keepdims=True)


def _sb_weights(ls, later):
    return jnp.exp2(ls + later).astype(bf16)


def _sb_prompt_body(bias_ref, q_ref, k_ref, v_ref, u_ref, o_ref,
                    kx_scr, qs_scr, zp_scr, c_scr, ls_scr, loc_scr, a_scr, acc_scr, r_scr, *, tq, tk):
    pr = pl.program_id(1)
    i = pl.program_id(2)
    nk = pl.num_programs(2) * (tq // tk)
    lane = lax.broadcasted_iota(jnp.int32, (1, LANES), 1)

    @pl.when(i == 0)
    def _():
        kx_scr[:, 0:LANES] = k_ref[...]
        kx_scr[:, LANES:2 * LANES] = jnp.broadcast_to(jnp.where(lane < 2, 1.0, 0.0), (kx_scr.shape[0], LANES)
                                                      ).astype(bf16)

    q = q_ref[...]
    zero_q = jnp.zeros_like(q)
    qs_scr[0:tq, 0:LANES] = jnp.where(lane < SB_DH, q, zero_q)
    qs_scr[tq:2 * tq, 0:LANES] = jnp.where(lane >= SB_DH, q, zero_q)
    row = lax.broadcasted_iota(jnp.int32, (2 * tq, 1), 0)
    nbias = jnp.where(row < tq, -LOG2E * bias_ref[2 * pr], -LOG2E * bias_ref[2 * pr + 1])
    nb_hi = nbias.astype(bf16).astype(f32)
    qs_scr[:, LANES:2 * LANES] = jnp.where(lane == 0, nb_hi, jnp.where(lane == 1, nbias - nb_hi, 0.0)).astype(bf16)
    t_row = lax.broadcasted_iota(jnp.int32, (2 * tq, tk), 0) % tq
    s_col = lax.broadcasted_iota(jnp.int32, (2 * tq, tk), 1)
    top = 2 * i + 1

    def kblock(ref, kb):
        return ref[pl.ds(pl.multiple_of(kb * tk, tk), tk), :]

    def scores(kb):
        return _dot_nt(qs_scr[...], kblock(kx_scr, jnp.maximum(kb, 0)))

    c0, ls0, r0 = _sb_logs(scores(top), jnp.zeros((2 * tq, 1), f32), s_col + tk < t_row)
    c_scr[0] = c0
    ls_scr[0] = ls0
    r_scr[...] = r0
    zp_scr[1] = scores(top - 1)
    a_scr[...] = jnp.zeros_like(a_scr)
    acc_scr[...] = jnp.zeros_like(acc_scr)

    def step(n, slot, strict):
        prev = 1 - slot
        acc_scr[...] += _dot(a_scr[...], kblock(v_ref, jnp.minimum(top - n + 2, nk - 1)))
        loc_scr[...] = _dot(c_scr[prev], u_ref[...])
        zp_scr[prev] = scores(top - n - 1)
        c, ls, r = _sb_logs(zp_scr[slot], r_scr[...], strict)
        c_scr[slot] = c
        ls_scr[slot] = ls
        r_scr[...] = r
        a_scr[...] = _sb_weights(ls_scr[prev], loc_scr[...])

    step(1, 1, s_col < t_row)

    def pair(j, carry):
        step(2 * j + 2, 0, None)
        step(2 * j + 3, 1, None)
        return carry

    lax.fori_loop(0, i, pair, 0)
    acc = acc_scr[...] + _dot(a_scr[...], kblock(v_ref, 1))
    a_last = _sb_weights(ls_scr[1], _dot(c_scr[1], u_ref[...]))
    acc = acc + _dot(a_last, kblock(v_ref, 0))
    o_ref[...] = jnp.where(lane < SB_DH, acc[:tq], acc[tq:]).astype(o_ref.dtype)


def _strict_upper(n):
    r = np.arange(n)
    return jnp.asarray((r[:, None] > r[None, :]).astype(np.float32), bf16)


def _sb_prompt(qn, kb, vb, bias, n_seq, seq_len, tk):
    tq = 2 * tk
    nq = seq_len // tq
    u = _strict_upper(tk)
    return pl.pallas_call(
        functools.partial(_sb_prompt_body, tq=tq, tk=tk),
        grid=(n_seq, SB_W // LANES, nq),
        in_specs=[pl.BlockSpec(memory_space=pltpu.SMEM),
                  pl.BlockSpec((tq, LANES), lambda b, p, i: (b * nq + i, p)),
                  pl.BlockSpec((seq_len, LANES), lambda b, p, i: (b, p)),
                  pl.BlockSpec((seq_len, LANES), lambda b, p, i: (b, p)),
                  _const_spec(u.shape)],
        out_specs=pl.BlockSpec((tq, LANES), lambda b, p, i: (b * nq + i, p)),
        out_shape=jax.ShapeDtypeStruct((n_seq * seq_len, SB_W), bf16),
        scratch_shapes=[pltpu.VMEM((seq_len, 2 * LANES), bf16), pltpu.VMEM((2 * tq, 2 * LANES), bf16),
                        pltpu.VMEM((2, 2 * tq, tk), f32), pltpu.VMEM((2, 2 * tq, tk), bf16),
                        pltpu.VMEM((2, 2 * tq, tk), f32), pltpu.VMEM((2 * tq, tk), f32),
                        pltpu.VMEM((2 * tq, tk), bf16), pltpu.VMEM((2 * tq, LANES), f32),
                        pltpu.VMEM((2 * tq, 1), f32)],
        compiler_params=_cparams(("arbitrary", "arbitrary", "arbitrary")),
        name="sb_prompt",
    )(bias, qn, kb, vb, u)


def _sb_sample_body(pt_ref, bias_ref, q_ref, kn_ref, vn_ref, u_ref, *rest, n_pages, dec):
    k_refs = rest[:n_pages]
    v_refs = rest[n_pages:2 * n_pages]
    o_ref = rest[2 * n_pages]
    rows = SB_HEADS * dec
    lane_head = lax.broadcasted_iota(jnp.int32, (1, SB_W), 1) // SB_DH
    q = q_ref[0].astype(f32)
    qbd = jnp.concatenate([jnp.where(lane_head == h, q, 0.0) for h in range(SB_HEADS)], axis=0).astype(bf16)
    row_head = lax.broadcasted_iota(jnp.int32, (rows, 1), 0) // dec
    nbias = jnp.zeros((rows, 1), f32)
    for h in range(SB_HEADS):
        nbias = jnp.where(row_head == h, -LOG2E * bias_ref[h], nbias)
    pad = jnp.zeros((PAGE - dec, SB_W), f32)
    kn = jnp.concatenate([kn_ref[0].astype(f32), pad], axis=0).astype(bf16)
    vn = jnp.concatenate([vn_ref[0].astype(f32), pad], axis=0).astype(bf16)
    t_row = lax.broadcasted_iota(jnp.int32, (rows, PAGE), 0) % dec
    s_col = lax.broadcasted_iota(jnp.int32, (rows, PAGE), 1)

    def pair(refs, g):
        return jnp.concatenate([refs[2 * g][0], refs[2 * g + 1][0]], axis=1).astype(bf16)

    groups = list(range(n_pages // 2 - 1, -1, -1))
    scores = [_dot_nt(qbd, kn)] + [_dot(qbd, pair(k_refs, g)) for g in groups]
    r = jnp.zeros((rows, 1), f32)
    cs, lss = [], []
    for n, zp in enumerate(scores):
        c, ls, r = _sb_logs(zp + nbias, r, (s_col < t_row) if n == 0 else None)
        cs.append(c)
        lss.append(ls)
    later = [_dot(cs[0], u_ref[0:PAGE, 0:PAGE])] + [_dot(c, u_ref[...]) for c in cs[1:]]
    acc = _dot(_sb_weights(lss[0], later[0]), vn)
    for n, g in enumerate(groups):
        acc = acc + _dot_nt(_sb_weights(lss[n + 1], later[n + 1]), pair(v_refs, g))
    out = jnp.zeros((dec, SB_W), f32)
    for h in range(SB_HEADS):
        out = out + jnp.where(lane_head == h, acc[h * dec:(h + 1) * dec, :], 0.0)
    o_ref[0] = out.astype(o_ref.dtype)


def _sb_sample(qn, kb, vb, cache_k, cache_v, page_table, bias, dec):
    n_seq, n_pages = page_table.shape
    assert n_pages % 2 == 0, "pages are consumed two at a time"
    u = _strict_upper(2 * PAGE)
    pt = page_table.reshape(-1)
    tok = pl.BlockSpec((1, dec, SB_W), lambda n, pt: (n, 0, 0))

    def page_spec(j):
        return pl.BlockSpec((1, SB_W, PAGE), lambda n, pt: (pt[n * n_pages + j], 0, 0))

    grid_spec = pltpu.PrefetchScalarGridSpec(
        num_scalar_prefetch=1,
        grid=(n_seq,),
        in_specs=[pl.BlockSpec(memory_space=pltpu.SMEM), tok, tok, tok,
                  pl.BlockSpec(u.shape, lambda n, pt: (0, 0))]
                 + [page_spec(j) for j in range(n_pages)] * 2,
        out_specs=tok,
    )
    return pl.pallas_call(
        functools.partial(_sb_sample_body, n_pages=n_pages, dec=dec),
        grid_spec=grid_spec,
        out_shape=jax.ShapeDtypeStruct((n_seq, dec, SB_W), bf16),
        compiler_params=_cparams(("arbitrary",)),
        name="sb_sample",
    )(pt, bias, qn.reshape(n_seq, dec, SB_W), kb.reshape(n_seq, dec, SB_W), vb.reshape(n_seq, dec, SB_W), u,
      *([cache_k] * n_pages), *([cache_v] * n_pages))


def _mix_out_body(x_ref, ohg_ref, osb_ref, wo_ref, g1_ref, g2_ref, wq_ref, x1_ref, qc_ref):
    mixed = _dot(ohg_ref[...], wo_ref[0:HG_W, :]) + _dot(osb_ref[...], wo_ref[HG_W:HG_W + SB_W, :])
    x1 = x_ref[...] + _rms(mixed, g1_ref[...])
    x1_ref[...] = x1
    h2 = _rms(x1, g2_ref[...]).astype(bf16)
    qc_ref[...] = (_dot(h2, wq_ref[...]) * CA_SCALE).astype(bf16)


def _mix_out(x, ohg, osb, wo_bf, g1, g2, wq_bf, tm):
    t = x.shape[0]
    row = lambda n: pl.BlockSpec((tm, n), lambda i: (i, 0))
    return pl.pallas_call(
        _mix_out_body,
        grid=(t // tm,),
        in_specs=[row(D_MODEL), row(HG_W), row(SB_W), _const_spec(wo_bf.shape),
                  _const_spec((1, D_MODEL)), _const_spec((1, D_MODEL)), _const_spec(wq_bf.shape)],
        out_specs=[row(D_MODEL), row(D_MODEL)],
        out_shape=[jax.ShapeDtypeStruct((t, D_MODEL), f32), jax.ShapeDtypeStruct((t, D_MODEL), bf16)],
        compiler_params=_cparams(("arbitrary",)),
        name="mix_out",
    )(x, ohg, osb, wo_bf, g1, g2, wq_bf)


def _mem_kv_body(m_ref, g_ref, wk_ref, wv_ref, mk_ref, mv_ref):
    mn = _rms(m_ref[...], g_ref[...]).astype(bf16)
    mk_ref[...] = _dot(mn, wk_ref[...])
    mv_ref[...] = _dot(mn, wv_ref[...])


def _mem_kv(mem, g, wk_bf, wv_bf, tm):
    t = mem.shape[0]
    row = pl.BlockSpec((tm, D_MODEL), lambda i: (i, 0))
    return pl.pallas_call(
        _mem_kv_body,
        grid=(t // tm,),
        in_specs=[row, _const_spec((1, D_MODEL)), _const_spec(wk_bf.shape), _const_spec(wv_bf.shape)],
        out_specs=[row, row],
        out_shape=[jax.ShapeDtypeStruct((t, D_MODEL), f32)] * 2,
        compiler_params=_cparams(("arbitrary",)),
        name="mem_kv",
    )(mem, g, wk_bf, wv_bf)


def _cross_heads(q, mk_head, mv_head):
    scores = [_dot_nt(q[:, h * CA_DH:(h + 1) * CA_DH], mk_head(h)) for h in range(CA_HEADS)]
    probs = []
    for s in scores:
        e = jnp.exp(s - jnp.max(s, axis=-1, keepdims=True))
        probs.append((e / jnp.sum(e, axis=-1, keepdims=True)).astype(bf16))
    return jnp.concatenate([_dot(p, mv_head(h)) for h, p in enumerate(probs)], axis=1)


def _cross_sample_body(q_ref, mk_hbm, mv_hbm, o_ref, kbuf, vbuf, sem, *, group, dec):
    i = pl.program_id(0)
    n_steps = pl.num_programs(0)

    def copies(step, slot):
        out = []
        for s in range(group):
            for h in range(CA_HEADS):
                n = step * group + s
                out.append(pltpu.make_async_copy(mk_hbm.at[n, :, h, :], kbuf.at[slot, s, h], sem.at[slot]))
                out.append(pltpu.make_async_copy(mv_hbm.at[n, :, h, :], vbuf.at[slot, s, h], sem.at[slot]))
        return out

    slot = i % 2

    @pl.when(i == 0)
    def _():
        for cp in copies(0, 0):
            cp.start()

    @pl.when(i + 1 < n_steps)
    def _():
        for cp in copies(i + 1, 1 - slot):
            cp.start()

    for cp in copies(i, slot):
        cp.wait()
    scores = [_dot_nt(q_ref[s * dec:(s + 1) * dec, h * CA_DH:(h + 1) * CA_DH], kbuf[slot, s, h].astype(bf16))
              for s in range(group) for h in range(CA_HEADS)]
    sc = jnp.concatenate(scores, axis=0)
    e = jnp.exp(sc - jnp.max(sc, axis=-1, keepdims=True))
    p = e / jnp.sum(e, axis=-1, keepdims=True)
    for s in range(group):
        outs = [_dot(p[(s * CA_HEADS + h) * dec:(s * CA_HEADS + h + 1) * dec, :].astype(bf16),
                     vbuf[slot, s, h].astype(bf16)) for h in range(CA_HEADS)]
        o_ref[s * dec:(s + 1) * dec, :] = jnp.concatenate(outs, axis=1).astype(o_ref.dtype)


def _cross_sample(qc, mk, mv, dec, group):
    n_seq = mk.shape[0]
    row = pl.BlockSpec((group * dec, D_MODEL), lambda i: (i, 0))
    buf = pltpu.VMEM((2, group, CA_HEADS, N_MEM, CA_DH), f32)
    return pl.pallas_call(
        functools.partial(_cross_sample_body, group=group, dec=dec),
        grid=(n_seq // group,),
        in_specs=[row, pl.BlockSpec(memory_space=pl.ANY), pl.BlockSpec(memory_space=pl.ANY)],
        out_specs=row,
        out_shape=jax.ShapeDtypeStruct(qc.shape, bf16),
        scratch_shapes=[buf, buf, pltpu.SemaphoreType.DMA((2,))],
        compiler_params=_cparams(("arbitrary",)),
        name="cross_sample",
    )(qc, mk, mv)


def _ca_out_body(ca_ref, x1_ref, wco_ref, g1_ref, g2_ref, x2_ref, h3_ref):
    x2 = x1_ref[...] + _rms(_dot(ca_ref[...], wco_ref[...]), g1_ref[...])
    x2_ref[...] = x2
    h3_ref[...] = _rms(x2, g2_ref[...]).astype(bf16)


def _ca_out(ca, x1, wco_bf, g1, g2, tm):
    t = x1.shape[0]
    row = pl.BlockSpec((tm, D_MODEL), lambda i: (i, 0))
    return pl.pallas_call(
        _ca_out_body,
        grid=(t // tm,),
        in_specs=[row, row, _const_spec(wco_bf.shape), _const_spec((1, D_MODEL)), _const_spec((1, D_MODEL))],
        out_specs=[row, row],
        out_shape=[jax.ShapeDtypeStruct((t, D_MODEL), f32), jax.ShapeDtypeStruct((t, D_MODEL), bf16)],
        compiler_params=_cparams(("arbitrary",)),
        name="ca_out",
    )(ca, x1, wco_bf, g1, g2)


FFN_CHUNK = 256


def _ffn_core(h3, x2, wup_ref, cw_ref, cb_ref, wdn_ref, g_ref, y_ref, act_scr, tm, *, tail=None, halo=None):
    if halo is not None:
        pos = lax.broadcasted_iota(jnp.int32, (tm, 1), 0) % halo[2]

    def conv_chunk(col):
        cs = slice(col, col + FFN_CHUNK)
        u = _dot(h3, wup_ref[:, cs])
        if tail is not None:
            tail_ref, ext_scr = tail
            ext_scr[8:8 + tm, cs] = u
            u1 = ext_scr[7:7 + tm, cs]
            u2 = ext_scr[6:6 + tm, cs]
            ext_scr[0:8, cs] = u[tm - 8:tm]
            tail_ref[:, cs] = u[tm - 8:tm]
        else:
            halo_ref, u_ref, _ = halo
            ext = jnp.concatenate([jnp.zeros((8, FFN_CHUNK), f32), u], axis=0)
            hal = halo_ref[:, cs]
            hext = jnp.concatenate([hal, jnp.zeros((8, FFN_CHUNK), f32)], axis=0)
            u1 = jnp.where(pos == 0, hext[1:1 + tm], ext[7:7 + tm])
            u2 = jnp.where(pos < 2, hal, ext[6:6 + tm])
            u_ref[:, cs] = u
        return cb_ref[:, cs] + cw_ref[2:3, cs] * u + cw_ref[1:2, cs] * u1 + cw_ref[0:1, cs] * u2

    for cj in range(D_FF // FFN_CHUNK):
        gate = conv_chunk(cj * FFN_CHUNK)
        val = conv_chunk(D_FF + cj * FFN_CHUNK)
        act_scr[:, cj * FFN_CHUNK:(cj + 1) * FFN_CHUNK] = (jax.nn.gelu(gate, approximate=True) * val).astype(bf16)
    y = _dot(act_scr[...], wdn_ref[...])
    y_ref[...] = x2 + _rms(y, g_ref[...])


def _ffn_sample_body(h3_ref, x2_ref, halo_ref, wup_ref, cw_ref, cb_ref, wdn_ref, g_ref, y_ref, u_ref, act_scr,
                     *, tm, dec):
    _ffn_core(h3_ref[...], x2_ref[...], wup_ref, cw_ref, cb_ref, wdn_ref, g_ref, y_ref, act_scr, tm,
              halo=(halo_ref, u_ref, dec))


def _post_ffn_body(x_ref, ohg_ref, osb_ref, wo_ref, g1_ref, g2_ref, wq_ref, mk_ref, mv_ref, wco_ref, g3_ref, g4_ref,
                   wup_ref, cw_ref, cb_ref, wdn_ref, g5_ref, y_ref, tail_ref, ext_scr, act_scr, *, tm):
    @pl.when(pl.program_id(1) == 0)
    def _():
        ext_scr[0:8, :] = jnp.zeros((8, ext_scr.shape[1]), f32)

    mixed = _dot(ohg_ref[...], wo_ref[0:HG_W, :]) + _dot(osb_ref[...], wo_ref[HG_W:HG_W + SB_W, :])
    x1 = x_ref[...] + _rms(mixed, g1_ref[...])
    qc = (_dot(_rms(x1, g2_ref[...]).astype(bf16), wq_ref[...]) * CA_SCALE).astype(bf16)
    mk = mk_ref[...].astype(bf16)
    mv = mv_ref[...].astype(bf16)
    ca = _cross_heads(qc, lambda h: mk[:, h * CA_DH:(h + 1) * CA_DH],
                      lambda h: mv[:, h * CA_DH:(h + 1) * CA_DH]).astype(bf16)
    x2 = x1 + _rms(_dot(ca, wco_ref[...]), g3_ref[...])
    h3 = _rms(x2, g4_ref[...]).astype(bf16)
    _ffn_core(h3, x2, wup_ref, cw_ref, cb_ref, wdn_ref, g5_ref, y_ref, act_scr, tm, tail=(tail_ref, ext_scr))


def _post_ffn(x, ohg, osb, wo_bf, g1, g2, wq_bf, mk, mv, wco_bf, g3, g4, wup_bf, cw, cb, wdn_bf, g5,
              n_seq, seq_len, tm):
    nblk = seq_len // tm
    row = lambda n: pl.BlockSpec((tm, n), lambda b, j: (b * nblk + j, 0))
    mem = pl.BlockSpec((N_MEM, D_MODEL), lambda b, j: (b, 0))
    gain = _const_spec((1, D_MODEL))
    return pl.pallas_call(
        functools.partial(_post_ffn_body, tm=tm),
        grid=(n_seq, nblk),
        in_specs=[row(D_MODEL), row(HG_W), row(SB_W), _const_spec(wo_bf.shape), gain, gain,
                  _const_spec(wq_bf.shape), mem, mem, _const_spec(wco_bf.shape), gain, gain,
                  _const_spec(wup_bf.shape), _const_spec(cw.shape), _const_spec(cb.shape),
                  _const_spec(wdn_bf.shape), gain],
        out_specs=[row(D_MODEL), pl.BlockSpec((8, 2 * D_FF), lambda b, j: (b, 0))],
        out_shape=[jax.ShapeDtypeStruct(x.shape, f32), jax.ShapeDtypeStruct((n_seq * 8, 2 * D_FF), f32)],
        scratch_shapes=[pltpu.VMEM((tm + 8, 2 * D_FF), f32), pltpu.VMEM((tm, D_FF), bf16)],
        compiler_params=_cparams(("arbitrary", "arbitrary")),
        name="post_ffn",
    )(x, ohg, osb, wo_bf, g1, g2, wq_bf, mk, mv, wco_bf, g3, g4, wup_bf, cw, cb, wdn_bf, g5)


def _ffn_sample(h3, x2, halo, wup_bf, cw, cb, wdn_bf, g, dec, tm):
    t = h3.shape[0]
    row = pl.BlockSpec((tm, D_MODEL), lambda i: (i, 0))
    wide = pl.BlockSpec((tm, 2 * D_FF), lambda i: (i, 0))
    return pl.pallas_call(
        functools.partial(_ffn_sample_body, tm=tm, dec=dec),
        grid=(t // tm,),
        in_specs=[row, row, wide, _const_spec(wup_bf.shape), _const_spec(cw.shape), _const_spec(cb.shape),
                  _const_spec(wdn_bf.shape), _const_spec((1, D_MODEL))],
        out_specs=[row, wide],
        out_shape=[jax.ShapeDtypeStruct((t, D_MODEL), f32), jax.ShapeDtypeStruct((t, 2 * D_FF), f32)],
        scratch_shapes=[pltpu.VMEM((tm, D_FF), bf16)],
        compiler_params=_cparams(("arbitrary",)),
        name="ffn_sample",
    )(h3, x2, halo, wup_bf, cw, cb, wdn_bf, g)


def kernel(x_prompt, x_sample, cache_sb_k, cache_sb_v, state_hgrn, state_ffn_conv, cache_mem_k, cache_mem_v,
           page_table, mem_prompt, w_in, hg_norm, hg_lb, sb_bias, w_o, g_mix_pre, g_mix_post, g_ca_pre, g_ca_post,
           g_mem, w_cq, w_ck, w_cv, w_co, g_ffn_pre, g_ffn_post, w_up, conv_w, conv_b, w_down):
    n_p, seq_len, _ = x_prompt.shape
    n_d, dec, _ = x_sample.shape
    depth = w_in.shape[0]
    assert depth == 1, "single-layer step"
    assert dec >= 2 and dec % 8 == 0, "the conv tail is taken from the new rows"
    l = 0
    row = lambda a: a[l].reshape(1, -1)
    w_in_bf, w_o_bf, w_cq_bf, w_co_bf = (w[l].astype(bf16) for w in (w_in, w_o, w_cq, w_co))
    w_ck_bf, w_cv_bf, w_up_bf, w_dn_bf = (w[l].astype(bf16) for w in (w_ck, w_cv, w_up, w_down))
    lbp = hg_lb[l:l + 2]
    gn = row(hg_norm)
    bias = sb_bias[l]
    cb = row(conv_b)
    cw = conv_w[l]

    xp = x_prompt.reshape(n_p * seq_len, D_MODEL)
    xs = x_sample.reshape(n_d * dec, D_MODEL)

    hin, sk_t, sv_t, qn, kb, vb = _in_proj(xp, row(g_mix_pre), w_in_bf, 512, n_seq=n_p)
    ohg, s_p = _hgrn_prompt(hin, lbp, gn, n_p, seq_len, 256, 16)
    osb = _sb_prompt(qn, kb, vb, bias, n_p, seq_len, 256)
    mk_p, mv_p = _mem_kv(mem_prompt.reshape(n_p * N_MEM, D_MODEL), row(g_mem), w_ck_bf, w_cv_bf, 256)
    yp, tail_p = _post_ffn(xp, ohg, osb, w_o_bf, row(g_mix_post), row(g_ca_pre), w_cq_bf, mk_p, mv_p, w_co_bf,
                           row(g_ca_post), row(g_ffn_pre), w_up_bf, cw, cb, w_dn_bf, row(g_ffn_post),
                           n_p, seq_len, 512)

    ts = n_d * dec
    hin_s, sk_s, sv_s, qn_s, kb_s, vb_s = _in_proj(xs, row(g_mix_pre), w_in_bf, 512)
    ohg_s, s_s = _hgrn_sample(hin_s, lbp, gn, state_hgrn[l], dec, 16)
    n_pool = cache_sb_k.shape[1]
    pages_t = lambda c: jnp.transpose(c[l], (0, 2, 3, 1)).reshape(n_pool, SB_W, PAGE)
    osb_s = _sb_sample(qn_s, kb_s, vb_s, pages_t(cache_sb_k), pages_t(cache_sb_v), page_table, bias,
                       dec).reshape(ts, SB_W)
    x1_s, qc_s = _mix_out(xs, ohg_s, osb_s, w_o_bf, row(g_mix_post), row(g_ca_pre), w_cq_bf, 512)
    ca_s = _cross_sample(qc_s, cache_mem_k[l], cache_mem_v[l], dec, 4)
    x2_s, h3_s = _ca_out(ca_s, x1_s, w_co_bf, row(g_ca_post), row(g_ffn_pre), 512)
    halo = jnp.pad(state_ffn_conv[l], ((0, 0), (0, dec - 2), (0, 0))).reshape(ts, 2 * D_FF)
    ys, u_s = _ffn_sample(h3_s, x2_s, halo, w_up_bf, cw, cb, w_dn_bf, row(g_ffn_post), dec, 128)

    kv_out = lambda a: jnp.transpose(a.reshape(n_p, SB_HEADS, SB_DH, seq_len), (0, 3, 1, 2))[None]
    return (yp.reshape(n_p, seq_len, D_MODEL), ys.reshape(n_d, dec, D_MODEL),
            kv_out(sk_t), kv_out(sv_t),
            s_p[None],
            tail_p.reshape(n_p, 8, 2 * D_FF)[None, :, 6:8],
            mk_p.reshape(1, n_p, N_MEM, CA_HEADS, CA_DH), mv_p.reshape(1, n_p, N_MEM, CA_HEADS, CA_DH),
            sk_s.reshape(1, n_d, dec, SB_HEADS, SB_DH), sv_s.reshape(1, n_d, dec, SB_HEADS, SB_DH),
            s_s[None],
            u_s.reshape(n_d, dec, 2 * D_FF)[None, :, dec - 2:dec])
```

```python
import functools

import numpy as np
import jax
import jax.numpy as jnp
from jax import lax
from jax.experimental import pallas as pl
from jax.experimental.pallas import tpu as pltpu

f32 = jnp.float32
bf16 = jnp.bfloat16

D_MODEL = 1024
HG_HEADS = 8
HG_DK = 64
HG_W = HG_HEADS * HG_DK
SB_HEADS = 8
SB_DH = 64
SB_W = SB_HEADS * SB_DH
SB_SCALE = SB_DH ** -0.5
LOG2E = 1.4426950408889634
N_MEM = 256
CA_HEADS = 4
CA_DH = D_MODEL // CA_HEADS
CA_SCALE = CA_DH ** -0.5
D_FF = 2816
RMS_EPS = 1e-6
PAGE = 128
LANES = 128
SUB = 8
VMEM_LIMIT = 56 * 1024 * 1024


def _cparams(sem, flags=None):
    return pltpu.CompilerParams(dimension_semantics=sem, vmem_limit_bytes=VMEM_LIMIT, flags=flags)


def _const_spec(shape):
    nd = len(shape)
    return pl.BlockSpec(shape, lambda *_: (0,) * nd, pipeline_mode=pl.Buffered(1))


def _rms(x, g):
    return x * lax.rsqrt(jnp.mean(x * x, axis=-1, keepdims=True) + RMS_EPS) * g


def _dot(a, b):
    return jnp.dot(a, b, preferred_element_type=f32)


def _dot_nt(a, b):
    return lax.dot_general(a, b, (((1,), (1,)), ((), ())), preferred_element_type=f32)


def _dot_tn(a, b):
    return lax.dot_general(a, b, (((0,), (0,)), ((), ())), preferred_element_type=f32)


def _in_proj_body(x_ref, g_ref, w_ref, hin_ref, sk_ref, sv_ref, qn_ref, kb_ref, vb_ref, *, kv_transposed):
    xn = _rms(x_ref[...], g_ref[...]).astype(bf16)
    for j in range(4):
        hin_ref[:, j * HG_W:(j + 1) * HG_W] = _dot(xn, w_ref[:, j * HG_W:(j + 1) * HG_W])
    base = 4 * HG_W
    q = _dot(xn, w_ref[:, base:base + SB_W])
    qn_ref[...] = (q * (-SB_SCALE * LOG2E)).astype(bf16)
    k = _dot(xn, w_ref[:, base + SB_W:base + 2 * SB_W])
    kb_ref[...] = k.astype(bf16)
    v = _dot(xn, w_ref[:, base + 2 * SB_W:base + 3 * SB_W])
    vb_ref[...] = v.astype(bf16)
    if kv_transposed:
        sk_ref[0] = k.T
        sv_ref[0] = v.T
    else:
        sk_ref[...] = k
        sv_ref[...] = v


def _in_proj(x, g, w_bf, tm, n_seq=None):
    t = x.shape[0]
    d_in = w_bf.shape[1]
    row = lambda n: pl.BlockSpec((tm, n), lambda i: (i, 0))
    if n_seq is None:
        kv_spec, kv_shape = row(SB_W), jax.ShapeDtypeStruct((t, SB_W), f32)
    else:
        nblk = t // n_seq // tm
        kv_spec = pl.BlockSpec((1, SB_W, tm), lambda i: (i // nblk, 0, i % nblk))
        kv_shape = jax.ShapeDtypeStruct((n_seq, SB_W, t // n_seq), f32)
    return pl.pallas_call(
        functools.partial(_in_proj_body, kv_transposed=n_seq is not None),
        grid=(t // tm,),
        in_specs=[row(D_MODEL), _const_spec((1, D_MODEL)), _const_spec((D_MODEL, d_in))],
        out_specs=[row(4 * HG_W), kv_spec, kv_spec, row(SB_W), row(SB_W), row(SB_W)],
        out_shape=[jax.ShapeDtypeStruct((t, 4 * HG_W), f32), kv_shape, kv_shape,
                   jax.ShapeDtypeStruct((t, SB_W), bf16), jax.ShapeDtypeStruct((t, SB_W), bf16),
                   jax.ShapeDtypeStruct((t, SB_W), bf16)],
        compiler_params=_cparams(("arbitrary",)),
        name="in_proj",
    )(x, g, w_bf)


def _hgrn_consts(nb, c):
    r = np.arange(nb)
    same = (r[:, None] // c) == (r[None, :] // c)
    tri = same & (r[None, :] <= r[:, None])
    stack = np.concatenate([tri, same], axis=0).astype(np.float32)
    h = np.arange(HG_W // 2) // HG_DK
    bo = (h[:, None] == h[None, :]).astype(np.float32)
    return jnp.asarray(stack, bf16), jnp.asarray(bo, bf16)


def _head_sums(x, bo):
    h = HG_W // 2
    return jnp.concatenate([_dot(x[:, :h], bo), _dot(x[:, h:], bo)], axis=1)


def _hgrn_prep(hin_ref, lbp_ref, stack_ref, scr, nb):
    q_scr, e_scr, v_scr, b_scr, qe_scr, kt_scr, el_scr = scr
    a = lbp_ref[...]
    e = jnp.exp(a - jnp.max(a, axis=0, keepdims=True))
    lb = e[0:1] / jnp.sum(e, axis=0, keepdims=True)
    hq = hin_ref[:, 0:HG_W]
    f = lb + (1.0 - lb) * jax.nn.sigmoid(hin_ref[:, HG_W:2 * HG_W])
    g = jnp.log2(f)
    kk = 1.0 - f
    g1 = g.astype(bf16)
    r1 = g - g1.astype(f32)
    g2 = r1.astype(bf16)
    g3 = (r1 - g2.astype(f32)).astype(bf16)
    st = stack_ref[...]
    bb = _dot(st, g1) + _dot(st, g2) + _dot(st, g3)
    b = bb[:nb]
    btot = bb[nb:]
    q_scr[...] = hq
    e_scr[...] = b - jnp.log2(kk)
    v_scr[...] = hin_ref[:, 2 * HG_W:3 * HG_W]
    b_scr[...] = b
    qe_scr[...] = hq * jnp.exp2(b)
    kt_scr[...] = kk * jnp.exp2(btot - b)
    el_scr[...] = jnp.exp2(btot)


def _hgrn_chunk(scr, bo, r0, c, st_list):
    q_scr, e_scr, v_scr, b_scr, qe_scr, kt_scr, el_scr = scr
    rows = pl.ds(r0, c)
    b_c = b_scr[rows, :]
    q_c = q_scr[rows, :]
    e_c = e_scr[rows, :]
    v_c = v_scr[rows, :]
    qe_c = qe_scr[rows, :].astype(bf16)
    kt_c = kt_scr[rows, :].astype(bf16)
    el_c = el_scr[pl.ds(r0, 1), :]
    v_cb = v_c.astype(bf16)
    n_sub = c // SUB
    t_loc = lax.broadcasted_iota(jnp.int32, (SUB, 1), 0)
    pieces, index = [], {}
    for s in range(c):
        for tg in range(s // SUB, n_sub):
            sl = slice(tg * SUB, (tg + 1) * SUB)
            pc = q_c[sl, :] * jnp.exp2(b_c[sl, :] - e_c[s:s + 1, :])
            if tg == s // SUB:
                pc = jnp.where(t_loc >= s % SUB, pc, 0.0)
            index[(s, tg)] = len(pieces)
            pieces.append(pc)
    p = jnp.concatenate(pieces, axis=0).astype(bf16)
    pw = _head_sums(p, bo)
    groups = []
    for tg in range(n_sub):
        acc = None
        for s in range((tg + 1) * SUB):
            k = index[(s, tg)]
            term = pw[k * SUB:(k + 1) * SUB, :] * v_c[s:s + 1, :]
            acc = term if acc is None else acc + term
        groups.append(acc)
    o = jnp.concatenate(groups, axis=0)
    ri = lax.broadcasted_iota(jnp.int32, (LANES, LANES), 0) // HG_DK
    ci = lax.broadcasted_iota(jnp.int32, (LANES, LANES), 1) // HG_DK
    same_head = ri == ci
    o_parts, new_states = [], []
    for pr in range(HG_W // LANES):
        sl = slice(pr * LANES, (pr + 1) * LANES)
        st = st_list[pr]
        o_parts.append(_dot_nt(qe_c[:, sl], st.astype(bf16)))
        upd = _dot_tn(v_cb[:, sl], kt_c[:, sl])
        new_states.append(st * el_c[:, sl] + jnp.where(same_head, upd, 0.0))
    return o + jnp.concatenate(o_parts, axis=1), new_states


def _hgrn_finish(o, hin_ref, gn_ref, bo, o_ref):
    ms = _head_sums((o * o).astype(bf16), bo) * (1.0 / HG_DK)
    gate = hin_ref[:, 3 * HG_W:4 * HG_W]
    o_ref[...] = (o * lax.rsqrt(ms + RMS_EPS) * gn_ref[...] * (gate * jax.nn.sigmoid(gate))).astype(o_ref.dtype)


def _pair_state_out(st):
    t = st.T
    return t[0:HG_DK, 0:HG_DK], t[HG_DK:LANES, HG_DK:LANES]


def _hgrn_prompt_body(hin_ref, lbp_ref, gn_ref, stack_ref, bo_ref, o_ref, sout_ref,
                      st_scr, oacc_scr, *scr, nb, c):
    j = pl.program_id(1)

    @pl.when(j == 0)
    def _():
        st_scr[...] = jnp.zeros_like(st_scr)

    _hgrn_prep(hin_ref, lbp_ref, stack_ref, scr, nb)
    bo = bo_ref[...]

    def step(ci, carry):
        r0 = pl.multiple_of(ci * c, c)
        o, new_states = _hgrn_chunk(scr, bo, r0, c, [st_scr[pr] for pr in range(HG_W // LANES)])
        for pr, st in enumerate(new_states):
            st_scr[pr] = st
        oacc_scr[pl.ds(r0, c), :] = o
        return carry

    lax.fori_loop(0, nb // c, step, 0, unroll=16)
    _hgrn_finish(oacc_scr[...], hin_ref, gn_ref, bo, o_ref)

    @pl.when(j == pl.num_programs(1) - 1)
    def _():
        for pr in range(HG_W // LANES):
            sa, sb = _pair_state_out(st_scr[pr])
            sout_ref[0, 2 * pr] = sa
            sout_ref[0, 2 * pr + 1] = sb


def _hgrn_prompt(hin, lbp, gn, n_seq, seq_len, nb, c):
    stack, bo = _hgrn_consts(nb, c)
    nblk = seq_len // nb
    scr = [pltpu.VMEM((nb, HG_W), f32) for _ in range(7)]
    return pl.pallas_call(
        functools.partial(_hgrn_prompt_body, nb=nb, c=c),
        grid=(n_seq, nblk),
        in_specs=[pl.BlockSpec((nb, 4 * HG_W), lambda b, j: (b * nblk + j, 0)),
                  _const_spec(lbp.shape), _const_spec((1, HG_W)),
                  _const_spec(stack.shape), _const_spec(bo.shape)],
        out_specs=[pl.BlockSpec((nb, HG_W), lambda b, j: (b * nblk + j, 0)),
                   pl.BlockSpec((1, HG_HEADS, HG_DK, HG_DK), lambda b, j: (b, 0, 0, 0))],
        out_shape=[jax.ShapeDtypeStruct((n_seq * seq_len, HG_W), bf16),
                   jax.ShapeDtypeStruct((n_seq, HG_HEADS, HG_DK, HG_DK), f32)],
        scratch_shapes=[pltpu.VMEM((HG_W // LANES, LANES, LANES), f32), pltpu.VMEM((nb, HG_W), f32)] + scr,
        compiler_params=_cparams(("arbitrary", "arbitrary")),
        name="hgrn_prompt",
    )(hin, lbp, gn, stack, bo)


def _hgrn_sample_body(hin_ref, lbp_ref, gn_ref, stack_ref, bo_ref, s0_ref, o_ref, sout_ref,
                      oacc_scr, *scr, nb, c):
    _hgrn_prep(hin_ref, lbp_ref, stack_ref, scr, nb)
    bo = bo_ref[...]
    zero = jnp.zeros((HG_DK, HG_DK), f32)

    def step(ci, carry):
        r0 = pl.multiple_of(ci * c, c)
        states = []
        for pr in range(HG_W // LANES):
            sa = s0_ref[ci, 2 * pr]
            sb = s0_ref[ci, 2 * pr + 1]
            bd = jnp.concatenate([jnp.concatenate([sa, zero], axis=1),
                                  jnp.concatenate([zero, sb], axis=1)], axis=0)
            states.append(bd.T)
        o, new_states = _hgrn_chunk(scr, bo, r0, c, states)
        for pr, st in enumerate(new_states):
            sa, sb = _pair_state_out(st)
            sout_ref[ci, 2 * pr] = sa
            sout_ref[ci, 2 * pr + 1] = sb
        oacc_scr[pl.ds(r0, c), :] = o
        return carry

    lax.fori_loop(0, nb // c, step, 0, unroll=4)
    _hgrn_finish(oacc_scr[...], hin_ref, gn_ref, bo, o_ref)


def _hgrn_sample(hin, lbp, gn, s0, c, seqs_per_step):
    n_seq = s0.shape[0]
    nb = seqs_per_step * c
    stack, bo = _hgrn_consts(nb, c)
    scr = [pltpu.VMEM((nb, HG_W), f32) for _ in range(7)]
    st_spec = pl.BlockSpec((seqs_per_step, HG_HEADS, HG_DK, HG_DK), lambda i: (i, 0, 0, 0))
    return pl.pallas_call(
        functools.partial(_hgrn_sample_body, nb=nb, c=c),
        grid=(n_seq // seqs_per_step,),
        in_specs=[pl.BlockSpec((nb, 4 * HG_W), lambda i: (i, 0)),
                  _const_spec(lbp.shape), _const_spec((1, HG_W)),
                  _const_spec(stack.shape), _const_spec(bo.shape), st_spec],
        out_specs=[pl.BlockSpec((nb, HG_W), lambda i: (i, 0)), st_spec],
        out_shape=[jax.ShapeDtypeStruct((n_seq * c, HG_W), bf16),
                   jax.ShapeDtypeStruct(s0.shape, f32)],
        scratch_shapes=[pltpu.VMEM((nb, HG_W), f32)] + scr,
        compiler_params=_cparams(("arbitrary",)),
        name="hgrn_sample",
    )(hin, lbp, gn, stack, bo, s0)


NEG_BIG = -1e30


def _sb_logs(zp, r, strict):
    c = jnp.minimum(zp, 0.0) - jnp.log2(1.0 + jnp.exp2(-jnp.abs(zp)))
    ls = (c - zp) + r
    if strict is not None:
        c = jnp.where(strict, c, 0.0)
        ls = jnp.where(strict, ls, NEG_BIG)
    return c.astype(bf16), ls, r + jnp.sum(c, axis=-1, keepdims=True)


def _sb_weights(ls, later):
    return jnp.exp2(ls + later).astype(bf16)


def _sb_prompt_body(bias_ref, q_ref, k_ref, v_ref, u_ref, o_ref,
                    kx_scr, qs_scr, zp_scr, c_scr, ls_scr, loc_scr, a_scr, acc_scr, r_scr, *, tq, tk):
    pr = pl.program_id(1)
    i = pl.program_id(2)
    nk = pl.num_programs(2) * (tq // tk)
    lane = lax.broadcasted_iota(jnp.int32, (1, LANES), 1)

    @pl.when(i == 0)
    def _():
        kx_scr[:, 0:LANES] = k_ref[...]
        kx_scr[:, LANES:2 * LANES] = jnp.broadcast_to(jnp.where(lane < 2, 1.0, 0.0), (kx_scr.shape[0], LANES)
                                                      ).astype(bf16)

    q = q_ref[...]
    zero_q = jnp.zeros_like(q)
    qs_scr[0:tq, 0:LANES] = jnp.where(lane < SB_DH, q, zero_q)
    qs_scr[tq:2 * tq, 0:LANES] = jnp.where(lane >= SB_DH, q, zero_q)
    row = lax.broadcasted_iota(jnp.int32, (2 * tq, 1), 0)
    nbias = jnp.where(row < tq, -LOG2E * bias_ref[2 * pr], -LOG2E * bias_ref[2 * pr + 1])
    nb_hi = nbias.astype(bf16).astype(f32)
    qs_scr[:, LANES:2 * LANES] = jnp.where(lane == 0, nb_hi, jnp.where(lane == 1, nbias - nb_hi, 0.0)).astype(bf16)
    t_row = lax.broadcasted_iota(jnp.int32, (2 * tq, tk), 0) % tq
    s_col = lax.broadcasted_iota(jnp.int32, (2 * tq, tk), 1)
    top = 2 * i + 1

    def kblock(ref, kb):
        return ref[pl.ds(pl.multiple_of(kb * tk, tk), tk), :]

    def scores(kb):
        return _dot_nt(qs_scr[...], kblock(kx_scr, jnp.maximum(kb, 0)))

    c0, ls0, r0 = _sb_logs(scores(top), jnp.zeros((2 * tq, 1), f32), s_col + tk < t_row)
    c_scr[0] = c0
    ls_scr[0] = ls0
    r_scr[...] = r0
    zp_scr[1] = scores(top - 1)
    a_scr[...] = jnp.zeros_like(a_scr)
    acc_scr[...] = jnp.zeros_like(acc_scr)

    def step(n, slot, strict):
        prev = 1 - slot
        acc_scr[...] += _dot(a_scr[...], kblock(v_ref, jnp.minimum(top - n + 2, nk - 1)))
        loc_scr[...] = _dot(c_scr[prev], u_ref[...])
        zp_scr[prev] = scores(top - n - 1)
        c, ls, r = _sb_logs(zp_scr[slot], r_scr[...], strict)
        c_scr[slot] = c
        ls_scr[slot] = ls
        r_scr[...] = r
        a_scr[...] = _sb_weights(ls_scr[prev], loc_scr[...])

    step(1, 1, s_col < t_row)

    def pair(j, carry):
        step(2 * j + 2, 0, None)
        step(2 * j + 3, 1, None)
        return carry

    lax.fori_loop(0, i, pair, 0)
    acc = acc_scr[...] + _dot(a_scr[...], kblock(v_ref, 1))
    a_last = _sb_weights(ls_scr[1], _dot(c_scr[1], u_ref[...]))
    acc = acc + _dot(a_last, kblock(v_ref, 0))
    o_ref[...] = jnp.where(lane < SB_DH, acc[:tq], acc[tq:]).astype(o_ref.dtype)


def _strict_upper(n):
    r = np.arange(n)
    return jnp.asarray((r[:, None] > r[None, :]).astype(np.float32), bf16)


def _sb_prompt(qn, kb, vb, bias, n_seq, seq_len, tk):
    tq = 2 * tk
    nq = seq_len // tq
    u = _strict_upper(tk)
    return pl.pallas_call(
        functools.partial(_sb_prompt_body, tq=tq, tk=tk),
        grid=(n_seq, SB_W // LANES, nq),
        in_specs=[pl.BlockSpec(memory_space=pltpu.SMEM),
                  pl.BlockSpec((tq, LANES), lambda b, p, i: (b * nq + i, p)),
                  pl.BlockSpec((seq_len, LANES), lambda b, p, i: (b, p)),
                  pl.BlockSpec((seq_len, LANES), lambda b, p, i: (b, p)),
                  _const_spec(u.shape)],
        out_specs=pl.BlockSpec((tq, LANES), lambda b, p, i: (b * nq + i, p)),
        out_shape=jax.ShapeDtypeStruct((n_seq * seq_len, SB_W), bf16),
        scratch_shapes=[pltpu.VMEM((seq_len, 2 * LANES), bf16), pltpu.VMEM((2 * tq, 2 * LANES), bf16),
                        pltpu.VMEM((2, 2 * tq, tk), f32), pltpu.VMEM((2, 2 * tq, tk), bf16),
                        pltpu.VMEM((2, 2 * tq, tk), f32), pltpu.VMEM((2 * tq, tk), f32),
                        pltpu.VMEM((2 * tq, tk), bf16), pltpu.VMEM((2 * tq, LANES), f32),
                        pltpu.VMEM((2 * tq, 1), f32)],
        compiler_params=_cparams(("arbitrary", "arbitrary", "arbitrary")),
        name="sb_prompt",
    )(bias, qn, kb, vb, u)


def _sb_sample_body(pt_ref, bias_ref, q_ref, kn_ref, vn_ref, u_ref, *rest, n_pages, dec):
    k_refs = rest[:n_pages]
    v_refs = rest[n_pages:2 * n_pages]
    o_ref = rest[2 * n_pages]
    rows = SB_HEADS * dec
    lane_head = lax.broadcasted_iota(jnp.int32, (1, SB_W), 1) // SB_DH
    q = q_ref[0].astype(f32)
    qbd = jnp.concatenate([jnp.where(lane_head == h, q, 0.0) for h in range(SB_HEADS)], axis=0).astype(bf16)
    row_head = lax.broadcasted_iota(jnp.int32, (rows, 1), 0) // dec
    nbias = jnp.zeros((rows, 1), f32)
    for h in range(SB_HEADS):
        nbias = jnp.where(row_head == h, -LOG2E * bias_ref[h], nbias)
    pad = jnp.zeros((PAGE - dec, SB_W), f32)
    kn = jnp.concatenate([kn_ref[0].astype(f32), pad], axis=0).astype(bf16)
    vn = jnp.concatenate([vn_ref[0].astype(f32), pad], axis=0).astype(bf16)
    t_row = lax.broadcasted_iota(jnp.int32, (rows, PAGE), 0) % dec
    s_col = lax.broadcasted_iota(jnp.int32, (rows, PAGE), 1)

    def pair(refs, g):
        return jnp.concatenate([refs[2 * g][0], refs[2 * g + 1][0]], axis=1).astype(bf16)

    groups = list(range(n_pages // 2 - 1, -1, -1))
    scores = [_dot_nt(qbd, kn)] + [_dot(qbd, pair(k_refs, g)) for g in groups]
    r = jnp.zeros((rows, 1), f32)
    cs, lss = [], []
    for n, zp in enumerate(scores):
        c, ls, r = _sb_logs(zp + nbias, r, (s_col < t_row) if n == 0 else None)
        cs.append(c)
        lss.append(ls)
    later = [_dot(cs[0], u_ref[0:PAGE, 0:PAGE])] + [_dot(c, u_ref[...]) for c in cs[1:]]
    acc = _dot(_sb_weights(lss[0], later[0]), vn)
    for n, g in enumerate(groups):
        acc = acc + _dot_nt(_sb_weights(lss[n + 1], later[n + 1]), pair(v_refs, g))
    out = jnp.zeros((dec, SB_W), f32)
    for h in range(SB_HEADS):
        out = out + jnp.where(lane_head == h, acc[h * dec:(h + 1) * dec, :], 0.0)
    o_ref[0] = out.astype(o_ref.dtype)


def _sb_sample(qn, kb, vb, cache_k, cache_v, page_table, bias, dec):
    n_seq, n_pages = page_table.shape
    assert n_pages % 2 == 0, "pages are consumed two at a time"
    u = _strict_upper(2 * PAGE)
    pt = page_table.reshape(-1)
    tok = pl.BlockSpec((1, dec, SB_W), lambda n, pt: (n, 0, 0))

    def page_spec(j):
        return pl.BlockSpec((1, SB_W, PAGE), lambda n, pt: (pt[n * n_pages + j], 0, 0))

    grid_spec = pltpu.PrefetchScalarGridSpec(
        num_scalar_prefetch=1,
        grid=(n_seq,),
        in_specs=[pl.BlockSpec(memory_space=pltpu.SMEM), tok, tok, tok,
                  pl.BlockSpec(u.shape, lambda n, pt: (0, 0))]
                 + [page_spec(j) for j in range(n_pages)] * 2,
        out_specs=tok,
    )
    return pl.pallas_call(
        functools.partial(_sb_sample_body, n_pages=n_pages, dec=dec),
        grid_spec=grid_spec,
        out_shape=jax.ShapeDtypeStruct((n_seq, dec, SB_W), bf16),
        compiler_params=_cparams(("arbitrary",)),
        name="sb_sample",
    )(pt, bias, qn.reshape(n_seq, dec, SB_W), kb.reshape(n_seq, dec, SB_W), vb.reshape(n_seq, dec, SB_W), u,
      *([cache_k] * n_pages), *([cache_v] * n_pages))


def _mix_out_body(x_ref, ohg_ref, osb_ref, wo_ref, g1_ref, g2_ref, wq_ref, x1_ref, qc_ref):
    mixed = _dot(ohg_ref[...], wo_ref[0:HG_W, :]) + _dot(osb_ref[...], wo_ref[HG_W:HG_W + SB_W, :])
    x1 = x_ref[...] + _rms(mixed, g1_ref[...])
    x1_ref[...] = x1
    h2 = _rms(x1, g2_ref[...]).astype(bf16)
    qc_ref[...] = (_dot(h2, wq_ref[...]) * CA_SCALE).astype(bf16)


def _mix_out(x, ohg, osb, wo_bf, g1, g2, wq_bf, tm):
    t = x.shape[0]
    row = lambda n: pl.BlockSpec((tm, n), lambda i: (i, 0))
    return pl.pallas_call(
        _mix_out_body,
        grid=(t // tm,),
        in_specs=[row(D_MODEL), row(HG_W), row(SB_W), _const_spec(wo_bf.shape),
                  _const_spec((1, D_MODEL)), _const_spec((1, D_MODEL)), _const_spec(wq_bf.shape)],
        out_specs=[row(D_MODEL), row(D_MODEL)],
        out_shape=[jax.ShapeDtypeStruct((t, D_MODEL), f32), jax.ShapeDtypeStruct((t, D_MODEL), bf16)],
        compiler_params=_cparams(("arbitrary",)),
        name="mix_out",
    )(x, ohg, osb, wo_bf, g1, g2, wq_bf)


def _mem_kv_body(m_ref, g_ref, wk_ref, wv_ref, mk_ref, mv_ref):
    mn = _rms(m_ref[...], g_ref[...]).astype(bf16)
    mk_ref[...] = _dot(mn, wk_ref[...])
    mv_ref[...] = _dot(mn, wv_ref[...])


def _mem_kv(mem, g, wk_bf, wv_bf, tm):
    t = mem.shape[0]
    row = pl.BlockSpec((tm, D_MODEL), lambda i: (i, 0))
    return pl.pallas_call(
        _mem_kv_body,
        grid=(t // tm,),
        in_specs=[row, _const_spec((1, D_MODEL)), _const_spec(wk_bf.shape), _const_spec(wv_bf.shape)],
        out_specs=[row, row],
        out_shape=[jax.ShapeDtypeStruct((t, D_MODEL), f32)] * 2,
        compiler_params=_cparams(("arbitrary",)),
        name="mem_kv",
    )(mem, g, wk_bf, wv_bf)


def _cross_heads(q, mk_head, mv_head):
    scores = [_dot_nt(q[:, h * CA_DH:(h + 1) * CA_DH], mk_head(h)) for h in range(CA_HEADS)]
    probs = []
    for s in scores:
        e = jnp.exp(s - jnp.max(s, axis=-1, keepdims=True))
        probs.append((e / jnp.sum(e, axis=-1, keepdims=True)).astype(bf16))
    return jnp.concatenate([_dot(p, mv_head(h)) for h, p in enumerate(probs)], axis=1)


def _cross_sample_body(q_ref, mk_hbm, mv_hbm, o_ref, kbuf, vbuf, sem, *, group, dec):
    i = pl.program_id(0)
    n_steps = pl.num_programs(0)

    def copies(step, slot):
        out = []
        for s in range(group):
            for h in range(CA_HEADS):
                n = step * group + s
                out.append(pltpu.make_async_copy(mk_hbm.at[n, :, h, :], kbuf.at[slot, s, h], sem.at[slot]))
                out.append(pltpu.make_async_copy(mv_hbm.at[n, :, h, :], vbuf.at[slot, s, h], sem.at[slot]))
        return out

    slot = i % 2

    @pl.when(i == 0)
    def _():
        for cp in copies(0, 0):
            cp.start()

    @pl.when(i + 1 < n_steps)
    def _():
        for cp in copies(i + 1, 1 - slot):
            cp.start()

    for cp in copies(i, slot):
        cp.wait()
    scores = [_dot_nt(q_ref[s * dec:(s + 1) * dec, h * CA_DH:(h + 1) * CA_DH], kbuf[slot, s, h].astype(bf16))
              for s in range(group) for h in range(CA_HEADS)]
    sc = jnp.concatenate(scores, axis=0)
    e = jnp.exp(sc - jnp.max(sc, axis=-1, keepdims=True))
    p = e / jnp.sum(e, axis=-1, keepdims=True)
    for s in range(group):
        outs = [_dot(p[(s * CA_HEADS + h) * dec:(s * CA_HEADS + h + 1) * dec, :].astype(bf16),
                     vbuf[slot, s, h].astype(bf16)) for h in range(CA_HEADS)]
        o_ref[s * dec:(s + 1) * dec, :] = jnp.concatenate(outs, axis=1).astype(o_ref.dtype)


def _cross_sample(qc, mk, mv, dec, group):
    n_seq = mk.shape[0]
    row = pl.BlockSpec((group * dec, D_MODEL), lambda i: (i, 0))
    buf = pltpu.VMEM((2, group, CA_HEADS, N_MEM, CA_DH), f32)
    return pl.pallas_call(
        functools.partial(_cross_sample_body, group=group, dec=dec),
        grid=(n_seq // group,),
        in_specs=[row, pl.BlockSpec(memory_space=pl.ANY), pl.BlockSpec(memory_space=pl.ANY)],
        out_specs=row,
        out_shape=jax.ShapeDtypeStruct(qc.shape, bf16),
        scratch_shapes=[buf, buf, pltpu.SemaphoreType.DMA((2,))],
        compiler_params=_cparams(("arbitrary",)),
        name="cross_sample",
    )(qc, mk, mv)


def _ca_out_body(ca_ref, x1_ref, wco_ref, g1_ref, g2_ref, x2_ref, h3_ref):
    x2 = x1_ref[...] + _rms(_dot(ca_ref[...], wco_ref[...]), g1_ref[...])
    x2_ref[...] = x2
    h3_ref[...] = _rms(x2, g2_ref[...]).astype(bf16)


def _ca_out(ca, x1, wco_bf, g1, g2, tm):
    t = x1.shape[0]
    row = pl.BlockSpec((tm, D_MODEL), lambda i: (i, 0))
    return pl.pallas_call(
        _ca_out_body,
        grid=(t // tm,),
        in_specs=[row, row, _const_spec(wco_bf.shape), _const_spec((1, D_MODEL)), _const_spec((1, D_MODEL))],
        out_specs=[row, row],
        out_shape=[jax.ShapeDtypeStruct((t, D_MODEL), f32), jax.ShapeDtypeStruct((t, D_MODEL), bf16)],
        compiler_params=_cparams(("arbitrary",)),
        name="ca_out",
    )(ca, x1, wco_bf, g1, g2)


FFN_CHUNK = 256


def _ffn_core(h3, x2, wup_ref, cw_ref, cb_ref, wdn_ref, g_ref, y_ref, act_scr, tm, *, tail=None, halo=None):
    if halo is not None:
        pos = lax.broadcasted_iota(jnp.int32, (tm, 1), 0) % halo[2]

    def conv_chunk(col):
        cs = slice(col, col + FFN_CHUNK)
        u = _dot(h3, wup_ref[:, cs])
        if tail is not None:
            tail_ref, ext_scr = tail
            ext_scr[8:8 + tm, cs] = u
            u1 = ext_scr[7:7 + tm, cs]
            u2 = ext_scr[6:6 + tm, cs]
            ext_scr[0:8, cs] = u[tm - 8:tm]
            tail_ref[:, cs] = u[tm - 8:tm]
        else:
            halo_ref, u_ref, _ = halo
            ext = jnp.concatenate([jnp.zeros((8, FFN_CHUNK), f32), u], axis=0)
            hal = halo_ref[:, cs]
            hext = jnp.concatenate([hal, jnp.zeros((8, FFN_CHUNK), f32)], axis=0)
            u1 = jnp.where(pos == 0, hext[1:1 + tm], ext[7:7 + tm])
            u2 = jnp.where(pos < 2, hal, ext[6:6 + tm])
            u_ref[:, cs] = u
        return cb_ref[:, cs] + cw_ref[2:3, cs] * u + cw_ref[1:2, cs] * u1 + cw_ref[0:1, cs] * u2

    for cj in range(D_FF // FFN_CHUNK):
        gate = conv_chunk(cj * FFN_CHUNK)
        val = conv_chunk(D_FF + cj * FFN_CHUNK)
        act_scr[:, cj * FFN_CHUNK:(cj + 1) * FFN_CHUNK] = (jax.nn.gelu(gate, approximate=True) * val).astype(bf16)
    y = _dot(act_scr[...], wdn_ref[...])
    y_ref[...] = x2 + _rms(y, g_ref[...])


def _ffn_sample_body(h3_ref, x2_ref, halo_ref, wup_ref, cw_ref, cb_ref, wdn_ref, g_ref, y_ref, u_ref, act_scr,
                     *, tm, dec):
    _ffn_core(h3_ref[...], x2_ref[...], wup_ref, cw_ref, cb_ref, wdn_ref, g_ref, y_ref, act_scr, tm,
              halo=(halo_ref, u_ref, dec))


def _post_ffn_body(x_ref, ohg_ref, osb_ref, wo_ref, g1_ref, g2_ref, wq_ref, mk_ref, mv_ref, wco_ref, g3_ref, g4_ref,
                   wup_ref, cw_ref, cb_ref, wdn_ref, g5_ref, y_ref, tail_ref, ext_scr, act_scr, *, tm):
    @pl.when(pl.program_id(1) == 0)
    def _():
        ext_scr[0:8, :] = jnp.zeros((8, ext_scr.shape[1]), f32)

    mixed = _dot(ohg_ref[...], wo_ref[0:HG_W, :]) + _dot(osb_ref[...], wo_ref[HG_W:HG_W + SB_W, :])
    x1 = x_ref[...] + _rms(mixed, g1_ref[...])
    qc = (_dot(_rms(x1, g2_ref[...]).astype(bf16), wq_ref[...]) * CA_SCALE).astype(bf16)
    mk = mk_ref[...].astype(bf16)
    mv = mv_ref[...].astype(bf16)
    ca = _cross_heads(qc, lambda h: mk[:, h * CA_DH:(h + 1) * CA_DH],
                      lambda h: mv[:, h * CA_DH:(h + 1) * CA_DH]).astype(bf16)
    x2 = x1 + _rms(_dot(ca, wco_ref[...]), g3_ref[...])
    h3 = _rms(x2, g4_ref[...]).astype(bf16)
    _ffn_core(h3, x2, wup_ref, cw_ref, cb_ref, wdn_ref, g5_ref, y_ref, act_scr, tm, tail=(tail_ref, ext_scr))


def _post_ffn(x, ohg, osb, wo_bf, g1, g2, wq_bf, mk, mv, wco_bf, g3, g4, wup_bf, cw, cb, wdn_bf, g5,
              n_seq, seq_len, tm):
    nblk = seq_len // tm
    row = lambda n: pl.BlockSpec((tm, n), lambda b, j: (b * nblk + j, 0))
    mem = pl.BlockSpec((N_MEM, D_MODEL), lambda b, j: (b, 0))
    gain = _const_spec((1, D_MODEL))
    return pl.pallas_call(
        functools.partial(_post_ffn_body, tm=tm),
        grid=(n_seq, nblk),
        in_specs=[row(D_MODEL), row(HG_W), row(SB_W), _const_spec(wo_bf.shape), gain, gain,
                  _const_spec(wq_bf.shape), mem, mem, _const_spec(wco_bf.shape), gain, gain,
                  _const_spec(wup_bf.shape), _const_spec(cw.shape), _const_spec(cb.shape),
                  _const_spec(wdn_bf.shape), gain],
        out_specs=[row(D_MODEL), pl.BlockSpec((8, 2 * D_FF), lambda b, j: (b, 0))],
        out_shape=[jax.ShapeDtypeStruct(x.shape, f32), jax.ShapeDtypeStruct((n_seq * 8, 2 * D_FF), f32)],
        scratch_shapes=[pltpu.VMEM((tm + 8, 2 * D_FF), f32), pltpu.VMEM((tm, D_FF), bf16)],
        compiler_params=_cparams(("arbitrary", "arbitrary")),
        name="post_ffn",
    )(x, ohg, osb, wo_bf, g1, g2, wq_bf, mk, mv, wco_bf, g3, g4, wup_bf, cw, cb, wdn_bf, g5)


def _ffn_sample(h3, x2, halo, wup_bf, cw, cb, wdn_bf, g, dec, tm):
    t = h3.shape[0]
    row = pl.BlockSpec((tm, D_MODEL), lambda i: (i, 0))
    wide = pl.BlockSpec((tm, 2 * D_FF), lambda i: (i, 0))
    return pl.pallas_call(
        functools.partial(_ffn_sample_body, tm=tm, dec=dec),
        grid=(t // tm,),
        in_specs=[row, row, wide, _const_spec(wup_bf.shape), _const_spec(cw.shape), _const_spec(cb.shape),
                  _const_spec(wdn_bf.shape), _const_spec((1, D_MODEL))],
        out_specs=[row, wide],
        out_shape=[jax.ShapeDtypeStruct((t, D_MODEL), f32), jax.ShapeDtypeStruct((t, 2 * D_FF), f32)],
        scratch_shapes=[pltpu.VMEM((tm, D_FF), bf16)],
        compiler_params=_cparams(("arbitrary",)),
        name="ffn_sample",
    )(h3, x2, halo, wup_bf, cw, cb, wdn_bf, g)


def kernel(x_prompt, x_sample, cache_sb_k, cache_sb_v, state_hgrn, state_ffn_conv, cache_mem_k, cache_mem_v,
           page_table, mem_prompt, w_in, hg_norm, hg_lb, sb_bias, w_o, g_mix_pre, g_mix_post, g_ca_pre, g_ca_post,
           g_mem, w_cq, w_ck, w_cv, w_co, g_ffn_pre, g_ffn_post, w_up, conv_w, conv_b, w_down):
    n_p, seq_len, _ = x_prompt.shape
    n_d, dec, _ = x_sample.shape
    depth = w_in.shape[0]
    assert depth == 1, "single-layer step"
    assert dec >= 2 and dec % 8 == 0, "the conv tail is taken from the new rows"
    l = 0
    row = lambda a: a[l].reshape(1, -1)
    w_in_bf, w_o_bf, w_cq_bf, w_co_bf = (w[l].astype(bf16) for w in (w_in, w_o, w_cq, w_co))
    w_ck_bf, w_cv_bf, w_up_bf, w_dn_bf = (w[l].astype(bf16) for w in (w_ck, w_cv, w_up, w_down))
    lbp = hg_lb[l:l + 2]
    gn = row(hg_norm)
    bias = sb_bias[l]
    cb = row(conv_b)
    cw = conv_w[l]

    xp = x_prompt.reshape(n_p * seq_len, D_MODEL)
    xs = x_sample.reshape(n_d * dec, D_MODEL)

    hin, sk_t, sv_t, qn, kb, vb = _in_proj(xp, row(g_mix_pre), w_in_bf, 512, n_seq=n_p)
    ohg, s_p = _hgrn_prompt(hin, lbp, gn, n_p, seq_len, 256, 16)
    osb = _sb_prompt(qn, kb, vb, bias, n_p, seq_len, 256)
    mk_p, mv_p = _mem_kv(mem_prompt.reshape(n_p * N_MEM, D_MODEL), row(g_mem), w_ck_bf, w_cv_bf, 256)
    yp, tail_p = _post_ffn(xp, ohg, osb, w_o_bf, row(g_mix_post), row(g_ca_pre), w_cq_bf, mk_p, mv_p, w_co_bf,
                           row(g_ca_post), row(g_ffn_pre), w_up_bf, cw, cb, w_dn_bf, row(g_ffn_post),
                           n_p, seq_len, 512)

    ts = n_d * dec
    hin_s, sk_s, sv_s, qn_s, kb_s, vb_s = _in_proj(xs, row(g_mix_pre), w_in_bf, 512)
    ohg_s, s_s = _hgrn_sample(hin_s, lbp, gn, state_hgrn[l], dec, 16)
    n_pool = cache_sb_k.shape[1]
    pages_t = lambda c: jnp.transpose(c[l], (0, 2, 3, 1)).reshape(n_pool, SB_W, PAGE)
    osb_s = _sb_sample(qn_s, kb_s, vb_s, pages_t(cache_sb_k), pages_t(cache_sb_v), page_table, bias,
                       dec).reshape(ts, SB_W)
    x1_s, qc_s = _mix_out(xs, ohg_s, osb_s, w_o_bf, row(g_mix_post), row(g_ca_pre), w_cq_bf, 512)
    ca_s = _cross_sample(qc_s, cache_mem_k[l], cache_mem_v[l], dec, 4)
    x2_s, h3_s = _ca_out(ca_s, x1_s, w_co_bf, row(g_ca_post), row(g_ffn_pre), 512)
    halo = jnp.pad(state_ffn_conv[l], ((0, 0), (0, dec - 2), (0, 0))).reshape(ts, 2 * D_FF)
    ys, u_s = _ffn_sample(h3_s, x2_s, halo, w_up_bf, cw, cb, w_dn_bf, row(g_ffn_post), dec, 128)

    kv_out = lambda a: jnp.transpose(a.reshape(n_p, SB_HEADS, SB_DH, seq_len), (0, 3, 1, 2))[None]
    return (yp.reshape(n_p, seq_len, D_MODEL), ys.reshape(n_d, dec, D_MODEL),
            kv_out(sk_t), kv_out(sv_t),
            s_p[None],
            tail_p.reshape(n_p, 8, 2 * D_FF)[None, :, 6:8],
            mk_p.reshape(1, n_p, N_MEM, CA_HEADS, CA_DH), mv_p.reshape(1, n_p, N_MEM, CA_HEADS, CA_DH),
            sk_s.reshape(1, n_d, dec, SB_HEADS, SB_DH), sv_s.reshape(1, n_d, dec, SB_HEADS, SB_DH),
            s_s[None],
            u_s.reshape(n_d, dec, 2 * D_FF)[None, :, dec - 2:dec])
```

```python
import functools

import numpy as np
import jax
import jax.numpy as jnp
from jax import lax
from jax.experimental import pallas as pl
from jax.experimental.pallas import tpu as pltpu

f32 = jnp.float32
bf16 = jnp.bfloat16

D_MODEL = 1024
HG_HEADS = 8
HG_DK = 64
HG_W = HG_HEADS * HG_DK
SB_HEADS = 8
SB_DH = 64
SB_W = SB_HEADS * SB_DH
SB_SCALE = SB_DH ** -0.5
LOG2E = 1.4426950408889634
N_MEM = 256
CA_HEADS = 4
CA_DH = D_MODEL // CA_HEADS
CA_SCALE = CA_DH ** -0.5
D_FF = 2816
RMS_EPS = 1e-6
PAGE = 128
LANES = 128
SUB = 8
VMEM_LIMIT = 56 * 1024 * 1024


def _cparams(sem, flags=None):
    return pltpu.CompilerParams(dimension_semantics=sem, vmem_limit_bytes=VMEM_LIMIT, flags=flags)


def _const_spec(shape):
    nd = len(shape)
    return pl.BlockSpec(shape, lambda *_: (0,) * nd, pipeline_mode=pl.Buffered(1))


def _rms(x, g):
    return x * lax.rsqrt(jnp.mean(x * x, axis=-1, keepdims=True) + RMS_EPS) * g


def _dot(a, b):
    return jnp.dot(a, b, preferred_element_type=f32)


def _dot_nt(a, b):
    return lax.dot_general(a, b, (((1,), (1,)), ((), ())), preferred_element_type=f32)


def _dot_tn(a, b):
    return lax.dot_general(a, b, (((0,), (0,)), ((), ())), preferred_element_type=f32)


def _in_proj_body(x_ref, g_ref, w_ref, hin_ref, sk_ref, sv_ref, qn_ref, kb_ref, vb_ref, *, kv_transposed):
    xn = _rms(x_ref[...], g_ref[...]).astype(bf16)
    for j in range(4):
        hin_ref[:, j * HG_W:(j + 1) * HG_W] = _dot(xn, w_ref[:, j * HG_W:(j + 1) * HG_W])
    base = 4 * HG_W
    q = _dot(xn, w_ref[:, base:base + SB_W])
    qn_ref[...] = (q * (-SB_SCALE * LOG2E)).astype(bf16)
    k = _dot(xn, w_ref[:, base + SB_W:base + 2 * SB_W])
    kb_ref[...] = k.astype(bf16)
    v = _dot(xn, w_ref[:, base + 2 * SB_W:base + 3 * SB_W])
    vb_ref[...] = v.astype(bf16)
    if kv_transposed:
        sk_ref[0] = k.T
        sv_ref[0] = v.T
    else:
        sk_ref[...] = k
        sv_ref[...] = v


def _in_proj(x, g, w_bf, tm, n_seq=None):
    t = x.shape[0]
    d_in = w_bf.shape[1]
    row = lambda n: pl.BlockSpec((tm, n), lambda i: (i, 0))
    if n_seq is None:
        kv_spec, kv_shape = row(SB_W), jax.ShapeDtypeStruct((t, SB_W), f32)
    else:
        nblk = t // n_seq // tm
        kv_spec = pl.BlockSpec((1, SB_W, tm), lambda i: (i // nblk, 0, i % nblk))
        kv_shape = jax.ShapeDtypeStruct((n_seq, SB_W, t // n_seq), f32)
    return pl.pallas_call(
        functools.partial(_in_proj_body, kv_transposed=n_seq is not None),
        grid=(t // tm,),
        in_specs=[row(D_MODEL), _const_spec((1, D_MODEL)), _const_spec((D_MODEL, d_in))],
        out_specs=[row(4 * HG_W), kv_spec, kv_spec, row(SB_W), row(SB_W), row(SB_W)],
        out_shape=[jax.ShapeDtypeStruct((t, 4 * HG_W), f32), kv_shape, kv_shape,
                   jax.ShapeDtypeStruct((t, SB_W), bf16), jax.ShapeDtypeStruct((t, SB_W), bf16),
                   jax.ShapeDtypeStruct((t, SB_W), bf16)],
        compiler_params=_cparams(("arbitrary",)),
        name="in_proj",
    )(x, g, w_bf)


def _hgrn_consts(nb, c):
    r = np.arange(nb)
    same = (r[:, None] // c) == (r[None, :] // c)
    tri = same & (r[None, :] <= r[:, None])
    stack = np.concatenate([tri, same], axis=0).astype(np.float32)
    h = np.arange(HG_W // 2) // HG_DK
    bo = (h[:, None] == h[None, :]).astype(np.float32)
    return jnp.asarray(stack, bf16), jnp.asarray(bo, bf16)


def _head_sums(x, bo):
    h = HG_W // 2
    return jnp.concatenate([_dot(x[:, :h], bo), _dot(x[:, h:], bo)], axis=1)


def _hgrn_prep(hin_ref, lbp_ref, stack_ref, scr, nb):
    q_scr, e_scr, v_scr, b_scr, qe_scr, kt_scr, el_scr = scr
    a = lbp_ref[...]
    e = jnp.exp(a - jnp.max(a, axis=0, keepdims=True))
    lb = e[0:1] / jnp.sum(e, axis=0, keepdims=True)
    hq = hin_ref[:, 0:HG_W]
    f = lb + (1.0 - lb) * jax.nn.sigmoid(hin_ref[:, HG_W:2 * HG_W])
    g = jnp.log2(f)
    kk = 1.0 - f
    g1 = g.astype(bf16)
    r1 = g - g1.astype(f32)
    g2 = r1.astype(bf16)
    g3 = (r1 - g2.astype(f32)).astype(bf16)
    st = stack_ref[...]
    bb = _dot(st, g1) + _dot(st, g2) + _dot(st, g3)
    b = bb[:nb]
    btot = bb[nb:]
    q_scr[...] = hq
    e_scr[...] = b - jnp.log2(kk)
    v_scr[...] = hin_ref[:, 2 * HG_W:3 * HG_W]
    b_scr[...] = b
    qe_scr[...] = hq * jnp.exp2(b)
    kt_scr[...] = kk * jnp.exp2(btot - b)
    el_scr[...] = jnp.exp2(btot)


def _hgrn_chunk(scr, bo, r0, c, st_list):
    q_scr, e_scr, v_scr, b_scr, qe_scr, kt_scr, el_scr = scr
    rows = pl.ds(r0, c)
    b_c = b_scr[rows, :]
    q_c = q_scr[rows, :]
    e_c = e_scr[rows, :]
    v_c = v_scr[rows, :]
    qe_c = qe_scr[rows, :].astype(bf16)
    kt_c = kt_scr[rows, :].astype(bf16)
    el_c = el_scr[pl.ds(r0, 1), :]
    v_cb = v_c.astype(bf16)
    n_sub = c // SUB
    t_loc = lax.broadcasted_iota(jnp.int32, (SUB, 1), 0)
    pieces, index = [], {}
    for s in range(c):
        for tg in range(s // SUB, n_sub):
            sl = slice(tg * SUB, (tg + 1) * SUB)
            pc = q_c[sl, :] * jnp.exp2(b_c[sl, :] - e_c[s:s + 1, :])
            if tg == s // SUB:
                pc = jnp.where(t_loc >= s % SUB, pc, 0.0)
            index[(s, tg)] = len(pieces)
            pieces.append(pc)
    p = jnp.concatenate(pieces, axis=0).astype(bf16)
    pw = _head_sums(p, bo)
    groups = []
    for tg in range(n_sub):
        acc = None
        for s in range((tg + 1) * SUB):
            k = index[(s, tg)]
            term = pw[k * SUB:(k + 1) * SUB, :] * v_c[s:s + 1, :]
            acc = term if acc is None else acc + term
        groups.append(acc)
    o = jnp.concatenate(groups, axis=0)
    ri = lax.broadcasted_iota(jnp.int32, (LANES, LANES), 0) // HG_DK
    ci = lax.broadcasted_iota(jnp.int32, (LANES, LANES), 1) // HG_DK
    same_head = ri == ci
    o_parts, new_states = [], []
    for pr in range(HG_W // LANES):
        sl = slice(pr * LANES, (pr + 1) * LANES)
        st = st_list[pr]
        o_parts.append(_dot_nt(qe_c[:, sl], st.astype(bf16)))
        upd = _dot_tn(v_cb[:, sl], kt_c[:, sl])
        new_states.append(st * el_c[:, sl] + jnp.where(same_head, upd, 0.0))
    return o + jnp.concatenate(o_parts, axis=1), new_states


def _hgrn_finish(o, hin_ref, gn_ref, bo, o_ref):
    ms = _head_sums((o * o).astype(bf16), bo) * (1.0 / HG_DK)
    gate = hin_ref[:, 3 * HG_W:4 * HG_W]
    o_ref[...] = (o * lax.rsqrt(ms + RMS_EPS) * gn_ref[...] * (gate * jax.nn.sigmoid(gate))).astype(o_ref.dtype)


def _pair_state_out(st):
    t = st.T
    return t[0:HG_DK, 0:HG_DK], t[HG_DK:LANES, HG_DK:LANES]


def _hgrn_prompt_body(hin_ref, lbp_ref, gn_ref, stack_ref, bo_ref, o_ref, sout_ref,
                      st_scr, oacc_scr, *scr, nb, c):
    j = pl.program_id(1)

    @pl.when(j == 0)
    def _():
        st_scr[...] = jnp.zeros_like(st_scr)

    _hgrn_prep(hin_ref, lbp_ref, stack_ref, scr, nb)
    bo = bo_ref[...]

    def step(ci, carry):
        r0 = pl.multiple_of(ci * c, c)
        o, new_states = _hgrn_chunk(scr, bo, r0, c, [st_scr[pr] for pr in range(HG_W // LANES)])
        for pr, st in enumerate(new_states):
            st_scr[pr] = st
        oacc_scr[pl.ds(r0, c), :] = o
        return carry

    lax.fori_loop(0, nb // c, step, 0, unroll=16)
    _hgrn_finish(oacc_scr[...], hin_ref, gn_ref, bo, o_ref)

    @pl.when(j == pl.num_programs(1) - 1)
    def _():
        for pr in range(HG_W // LANES):
            sa, sb = _pair_state_out(st_scr[pr])
            sout_ref[0, 2 * pr] = sa
            sout_ref[0, 2 * pr + 1] = sb


def _hgrn_prompt(hin, lbp, gn, n_seq, seq_len, nb, c):
    stack, bo = _hgrn_consts(nb, c)
    nblk = seq_len // nb
    scr = [pltpu.VMEM((nb, HG_W), f32) for _ in range(7)]
    return pl.pallas_call(
        functools.partial(_hgrn_prompt_body, nb=nb, c=c),
        grid=(n_seq, nblk),
        in_specs=[pl.BlockSpec((nb, 4 * HG_W), lambda b, j: (b * nblk + j, 0)),
                  _const_spec(lbp.shape), _const_spec((1, HG_W)),
                  _const_spec(stack.shape), _const_spec(bo.shape)],
        out_specs=[pl.BlockSpec((nb, HG_W), lambda b, j: (b * nblk + j, 0)),
                   pl.BlockSpec((1, HG_HEADS, HG_DK, HG_DK), lambda b, j: (b, 0, 0, 0))],
        out_shape=[jax.ShapeDtypeStruct((n_seq * seq_len, HG_W), bf16),
                   jax.ShapeDtypeStruct((n_seq, HG_HEADS, HG_DK, HG_DK), f32)],
        scratch_shapes=[pltpu.VMEM((HG_W // LANES, LANES, LANES), f32), pltpu.VMEM((nb, HG_W), f32)] + scr,
        compiler_params=_cparams(("arbitrary", "arbitrary")),
        name="hgrn_prompt",
    )(hin, lbp, gn, stack, bo)


def _hgrn_sample_body(hin_ref, lbp_ref, gn_ref, stack_ref, bo_ref, s0_ref, o_ref, sout_ref,
                      oacc_scr, *scr, nb, c):
    _hgrn_prep(hin_ref, lbp_ref, stack_ref, scr, nb)
    bo = bo_ref[...]
    zero = jnp.zeros((HG_DK, HG_DK), f32)

    def step(ci, carry):
        r0 = pl.multiple_of(ci * c, c)
        states = []
        for pr in range(HG_W // LANES):
            sa = s0_ref[ci, 2 * pr]
            sb = s0_ref[ci, 2 * pr + 1]
            bd = jnp.concatenate([jnp.concatenate([sa, zero], axis=1),
                                  jnp.concatenate([zero, sb], axis=1)], axis=0)
            states.append(bd.T)
        o, new_states = _hgrn_chunk(scr, bo, r0, c, states)
        for pr, st in enumerate(new_states):
            sa, sb = _pair_state_out(st)
            sout_ref[ci, 2 * pr] = sa
            sout_ref[ci, 2 * pr + 1] = sb
        oacc_scr[pl.ds(r0, c), :] = o
        return carry

    lax.fori_loop(0, nb // c, step, 0, unroll=16)
    _hgrn_finish(oacc_scr[...], hin_ref, gn_ref, bo, o_ref)


def _hgrn_sample(hin, lbp, gn, s0, c, seqs_per_step):
    n_seq = s0.shape[0]
    nb = seqs_per_step * c
    stack, bo = _hgrn_consts(nb, c)
    scr = [pltpu.VMEM((nb, HG_W), f32) for _ in range(7)]
    st_spec = pl.BlockSpec((seqs_per_step, HG_HEADS, HG_DK, HG_DK), lambda i: (i, 0, 0, 0))
    return pl.pallas_call(
        functools.partial(_hgrn_sample_body, nb=nb, c=c),
        grid=(n_seq // seqs_per_step,),
        in_specs=[pl.BlockSpec((nb, 4 * HG_W), lambda i: (i, 0)),
                  _const_spec(lbp.shape), _const_spec((1, HG_W)),
                  _const_spec(stack.shape), _const_spec(bo.shape), st_spec],
        out_specs=[pl.BlockSpec((nb, HG_W), lambda i: (i, 0)), st_spec],
        out_shape=[jax.ShapeDtypeStruct((n_seq * c, HG_W), bf16),
                   jax.ShapeDtypeStruct(s0.shape, f32)],
        scratch_shapes=[pltpu.VMEM((nb, HG_W), f32)] + scr,
        compiler_params=_cparams(("arbitrary",)),
        name="hgrn_sample",
    )(hin, lbp, gn, stack, bo, s0)


NEG_BIG = -1e30


def _sb_logs(zp, r, strict):
    c = jnp.minimum(zp, 0.0) - jnp.log2(1.0 + jnp.exp2(-jnp.abs(zp)))
    ls = (c - zp) + r
    if strict is not None:
        c = jnp.where(strict, c, 0.0)
        ls = jnp.where(strict, ls, NEG_BIG)
    return c.astype(bf16), ls, r + jnp.sum(c, axis=-1, keepdims=True)


def _sb_weights(ls, later):
    return jnp.exp2(ls + later).astype(bf16)


def _sb_prompt_body(bias_ref, q_ref, k_ref, v_ref, u_ref, o_ref,
                    kx_scr, qs_scr, zp_scr, c_scr, ls_scr, loc_scr, a_scr, acc_scr, r_scr, *, tq, tk):
    pr = pl.program_id(1)
    i = pl.program_id(2)
    nk = pl.num_programs(2) * (tq // tk)
    lane = lax.broadcasted_iota(jnp.int32, (1, LANES), 1)

    @pl.when(i == 0)
    def _():
        kx_scr[:, 0:LANES] = k_ref[...]
        kx_scr[:, LANES:2 * LANES] = jnp.broadcast_to(jnp.where(lane < 2, 1.0, 0.0), (kx_scr.shape[0], LANES)
                                                      ).astype(bf16)

    q = q_ref[...]
    zero_q = jnp.zeros_like(q)
    qs_scr[0:tq, 0:LANES] = jnp.where(lane < SB_DH, q, zero_q)
    qs_scr[tq:2 * tq, 0:LANES] = jnp.where(lane >= SB_DH, q, zero_q)
    row = lax.broadcasted_iota(jnp.int32, (2 * tq, 1), 0)
    nbias = jnp.where(row < tq, -LOG2E * bias_ref[2 * pr], -LOG2E * bias_ref[2 * pr + 1])
    nb_hi = nbias.astype(bf16).astype(f32)
    qs_scr[:, LANES:2 * LANES] = jnp.where(lane == 0, nb_hi, jnp.where(lane == 1, nbias - nb_hi, 0.0)).astype(bf16)
    t_row = lax.broadcasted_iota(jnp.int32, (2 * tq, tk), 0) % tq
    s_col = lax.broadcasted_iota(jnp.int32, (2 * tq, tk), 1)
    top = 2 * i + 1

    def kblock(ref, kb):
        return ref[pl.ds(pl.multiple_of(kb * tk, tk), tk), :]

    def scores(kb):
        return _dot_nt(qs_scr[...], kblock(kx_scr, jnp.maximum(kb, 0)))

    c0, ls0, r0 = _sb_logs(scores(top), jnp.zeros((2 * tq, 1), f32), s_col + tk < t_row)
    c_scr[0] = c0
    ls_scr[0] = ls0
    r_scr[...] = r0
    zp_scr[1] = scores(top - 1)
    a_scr[...] = jnp.zeros_like(a_scr)
    acc_scr[...] = jnp.zeros_like(acc_scr)

    def step(n, slot, strict):
        prev = 1 - slot
        acc_scr[...] += _dot(a_scr[...], kblock(v_ref, jnp.minimum(top - n + 2, nk - 1)))
        loc_scr[...] = _dot(c_scr[prev], u_ref[...])
        zp_scr[prev] = scores(top - n - 1)
        c, ls, r = _sb_logs(zp_scr[slot], r_scr[...], strict)
        c_scr[slot] = c
        ls_scr[slot] = ls
        r_scr[...] = r
        a_scr[...] = _sb_weights(ls_scr[prev], loc_scr[...])

    step(1, 1, s_col < t_row)

    def pair(j, carry):
        step(2 * j + 2, 0, None)
        step(2 * j + 3, 1, None)
        return carry

    lax.fori_loop(0, i, pair, 0)
    acc = acc_scr[...] + _dot(a_scr[...], kblock(v_ref, 1))
    a_last = _sb_weights(ls_scr[1], _dot(c_scr[1], u_ref[...]))
    acc = acc + _dot(a_last, kblock(v_ref, 0))
    o_ref[...] = jnp.where(lane < SB_DH, acc[:tq], acc[tq:]).astype(o_ref.dtype)


def _strict_upper(n):
    r = np.arange(n)
    return jnp.asarray((r[:, None] > r[None, :]).astype(np.float32), bf16)


def _sb_prompt(qn, kb, vb, bias, n_seq, seq_len, tk):
    tq = 2 * tk
    nq = seq_len // tq
    u = _strict_upper(tk)
    return pl.pallas_call(
        functools.partial(_sb_prompt_body, tq=tq, tk=tk),
        grid=(n_seq, SB_W // LANES, nq),
        in_specs=[pl.BlockSpec(memory_space=pltpu.SMEM),
                  pl.BlockSpec((tq, LANES), lambda b, p, i: (b * nq + i, p)),
                  pl.BlockSpec((seq_len, LANES), lambda b, p, i: (b, p)),
                  pl.BlockSpec((seq_len, LANES), lambda b, p, i: (b, p)),
                  _const_spec(u.shape)],
        out_specs=pl.BlockSpec((tq, LANES), lambda b, p, i: (b * nq + i, p)),
        out_shape=jax.ShapeDtypeStruct((n_seq * seq_len, SB_W), bf16),
        scratch_shapes=[pltpu.VMEM((seq_len, 2 * LANES), bf16), pltpu.VMEM((2 * tq, 2 * LANES), bf16),
                        pltpu.VMEM((2, 2 * tq, tk), f32), pltpu.VMEM((2, 2 * tq, tk), bf16),
                        pltpu.VMEM((2, 2 * tq, tk), f32), pltpu.VMEM((2 * tq, tk), f32),
                        pltpu.VMEM((2 * tq, tk), bf16), pltpu.VMEM((2 * tq, LANES), f32),
                        pltpu.VMEM((2 * tq, 1), f32)],
        compiler_params=_cparams(("arbitrary", "arbitrary", "arbitrary")),
        name="sb_prompt",
    )(bias, qn, kb, vb, u)


def _sb_sample_body(pt_ref, bias_ref, q_ref, kn_ref, vn_ref, u_ref, *rest, n_pages, dec):
    k_refs = rest[:n_pages]
    v_refs = rest[n_pages:2 * n_pages]
    o_ref = rest[2 * n_pages]
    rows = SB_HEADS * dec
    lane_head = lax.broadcasted_iota(jnp.int32, (1, SB_W), 1) // SB_DH
    q = q_ref[0].astype(f32)
    qbd = jnp.concatenate([jnp.where(lane_head == h, q, 0.0) for h in range(SB_HEADS)], axis=0).astype(bf16)
    row_head = lax.broadcasted_iota(jnp.int32, (rows, 1), 0) // dec
    nbias = jnp.zeros((rows, 1), f32)
    for h in range(SB_HEADS):
        nbias = jnp.where(row_head == h, -LOG2E * bias_ref[h], nbias)
    pad = jnp.zeros((PAGE - dec, SB_W), f32)
    kn = jnp.concatenate([kn_ref[0].astype(f32), pad], axis=0).astype(bf16)
    vn = jnp.concatenate([vn_ref[0].astype(f32), pad], axis=0).astype(bf16)
    t_row = lax.broadcasted_iota(jnp.int32, (rows, PAGE), 0) % dec
    s_col = lax.broadcasted_iota(jnp.int32, (rows, PAGE), 1)

    def pair(refs, g):
        return jnp.concatenate([refs[2 * g][0], refs[2 * g + 1][0]], axis=1).astype(bf16)

    groups = list(range(n_pages // 2 - 1, -1, -1))
    scores = [_dot_nt(qbd, kn)] + [_dot(qbd, pair(k_refs, g)) for g in groups]
    r = jnp.zeros((rows, 1), f32)
    cs, lss = [], []
    for n, zp in enumerate(scores):
        c, ls, r = _sb_logs(zp + nbias, r, (s_col < t_row) if n == 0 else None)
        cs.append(c)
        lss.append(ls)
    later = [_dot(cs[0], u_ref[0:PAGE, 0:PAGE])] + [_dot(c, u_ref[...]) for c in cs[1:]]
    acc = _dot(_sb_weights(lss[0], later[0]), vn)
    for n, g in enumerate(groups):
        acc = acc + _dot_nt(_sb_weights(lss[n + 1], later[n + 1]), pair(v_refs, g))
    out = jnp.zeros((dec, SB_W), f32)
    for h in range(SB_HEADS):
        out = out + jnp.where(lane_head == h, acc[h * dec:(h + 1) * dec, :], 0.0)
    o_ref[0] = out.astype(o_ref.dtype)


def _sb_sample(qn, kb, vb, cache_k, cache_v, page_table, bias, dec):
    n_seq, n_pages = page_table.shape
    assert n_pages % 2 == 0, "pages are consumed two at a time"
    u = _strict_upper(2 * PAGE)
    pt = page_table.reshape(-1)
    tok = pl.BlockSpec((1, dec, SB_W), lambda n, pt: (n, 0, 0))

    def page_spec(j):
        return pl.BlockSpec((1, SB_W, PAGE), lambda n, pt: (pt[n * n_pages + j], 0, 0))

    grid_spec = pltpu.PrefetchScalarGridSpec(
        num_scalar_prefetch=1,
        grid=(n_seq,),
        in_specs=[pl.BlockSpec(memory_space=pltpu.SMEM), tok, tok, tok,
                  pl.BlockSpec(u.shape, lambda n, pt: (0, 0))]
                 + [page_spec(j) for j in range(n_pages)] * 2,
        out_specs=tok,
    )
    return pl.pallas_call(
        functools.partial(_sb_sample_body, n_pages=n_pages, dec=dec),
        grid_spec=grid_spec,
        out_shape=jax.ShapeDtypeStruct((n_seq, dec, SB_W), bf16),
        compiler_params=_cparams(("arbitrary",)),
        name="sb_sample",
    )(pt, bias, qn.reshape(n_seq, dec, SB_W), kb.reshape(n_seq, dec, SB_W), vb.reshape(n_seq, dec, SB_W), u,
      *([cache_k] * n_pages), *([cache_v] * n_pages))


def _mix_out_body(x_ref, ohg_ref, osb_ref, wo_ref, g1_ref, g2_ref, wq_ref, x1_ref, qc_ref):
    mixed = _dot(ohg_ref[...], wo_ref[0:HG_W, :]) + _dot(osb_ref[...], wo_ref[HG_W:HG_W + SB_W, :])
    x1 = x_ref[...] + _rms(mixed, g1_ref[...])
    x1_ref[...] = x1
    h2 = _rms(x1, g2_ref[...]).astype(bf16)
    qc_ref[...] = (_dot(h2, wq_ref[...]) * CA_SCALE).astype(bf16)


def _mix_out(x, ohg, osb, wo_bf, g1, g2, wq_bf, tm):
    t = x.shape[0]
    row = lambda n: pl.BlockSpec((tm, n), lambda i: (i, 0))
    return pl.pallas_call(
        _mix_out_body,
        grid=(t // tm,),
        in_specs=[row(D_MODEL), row(HG_W), row(SB_W), _const_spec(wo_bf.shape),
                  _const_spec((1, D_MODEL)), _const_spec((1, D_MODEL)), _const_spec(wq_bf.shape)],
        out_specs=[row(D_MODEL), row(D_MODEL)],
        out_shape=[jax.ShapeDtypeStruct((t, D_MODEL), f32), jax.ShapeDtypeStruct((t, D_MODEL), bf16)],
        compiler_params=_cparams(("arbitrary",)),
        name="mix_out",
    )(x, ohg, osb, wo_bf, g1, g2, wq_bf)


def _mem_kv_body(m_ref, g_ref, wk_ref, wv_ref, mk_ref, mv_ref):
    mn = _rms(m_ref[...], g_ref[...]).astype(bf16)
    mk_ref[...] = _dot(mn, wk_ref[...])
    mv_ref[...] = _dot(mn, wv_ref[...])


def _mem_kv(mem, g, wk_bf, wv_bf, tm):
    t = mem.shape[0]
    row = pl.BlockSpec((tm, D_MODEL), lambda i: (i, 0))
    return pl.pallas_call(
        _mem_kv_body,
        grid=(t // tm,),
        in_specs=[row, _const_spec((1, D_MODEL)), _const_spec(wk_bf.shape), _const_spec(wv_bf.shape)],
        out_specs=[row, row],
        out_shape=[jax.ShapeDtypeStruct((t, D_MODEL), f32)] * 2,
        compiler_params=_cparams(("arbitrary",)),
        name="mem_kv",
    )(mem, g, wk_bf, wv_bf)


def _cross_heads(q, mk_head, mv_head):
    scores = [_dot_nt(q[:, h * CA_DH:(h + 1) * CA_DH], mk_head(h)) for h in range(CA_HEADS)]
    probs = []
    for s in scores:
        e = jnp.exp(s - jnp.max(s, axis=-1, keepdims=True))
        probs.append((e / jnp.sum(e, axis=-1, keepdims=True)).astype(bf16))
    return jnp.concatenate([_dot(p, mv_head(h)) for h, p in enumerate(probs)], axis=1)


def _cross_sample_body(q_ref, mk_hbm, mv_hbm, o_ref, kbuf, vbuf, sem, *, group, dec):
    i = pl.program_id(0)
    n_steps = pl.num_programs(0)

    def copies(step, slot):
        out = []
        for s in range(group):
            for h in range(CA_HEADS):
                n = step * group + s
                out.append(pltpu.make_async_copy(mk_hbm.at[n, :, h, :], kbuf.at[slot, s, h], sem.at[slot]))
                out.append(pltpu.make_async_copy(mv_hbm.at[n, :, h, :], vbuf.at[slot, s, h], sem.at[slot]))
        return out

    slot = i % 2

    @pl.when(i == 0)
    def _():
        for cp in copies(0, 0):
            cp.start()

    @pl.when(i + 1 < n_steps)
    def _():
        for cp in copies(i + 1, 1 - slot):
            cp.start()

    for cp in copies(i, slot):
        cp.wait()
    scores = [_dot_nt(q_ref[s * dec:(s + 1) * dec, h * CA_DH:(h + 1) * CA_DH], kbuf[slot, s, h].astype(bf16))
              for s in range(group) for h in range(CA_HEADS)]
    sc = jnp.concatenate(scores, axis=0)
    e = jnp.exp(sc - jnp.max(sc, axis=-1, keepdims=True))
    p = e / jnp.sum(e, axis=-1, keepdims=True)
    for s in range(group):
        outs = [_dot(p[(s * CA_HEADS + h) * dec:(s * CA_HEADS + h + 1) * dec, :].astype(bf16),
                     vbuf[slot, s, h].astype(bf16)) for h in range(CA_HEADS)]
        o_ref[s * dec:(s + 1) * dec, :] = jnp.concatenate(outs, axis=1).astype(o_ref.dtype)


def _cross_sample(qc, mk, mv, dec, group):
    n_seq = mk.shape[0]
    row = pl.BlockSpec((group * dec, D_MODEL), lambda i: (i, 0))
    buf = pltpu.VMEM((2, group, CA_HEADS, N_MEM, CA_DH), f32)
    return pl.pallas_call(
        functools.partial(_cross_sample_body, group=group, dec=dec),
        grid=(n_seq // group,),
        in_specs=[row, pl.BlockSpec(memory_space=pl.ANY), pl.BlockSpec(memory_space=pl.ANY)],
        out_specs=row,
        out_shape=jax.ShapeDtypeStruct(qc.shape, bf16),
        scratch_shapes=[buf, buf, pltpu.SemaphoreType.DMA((2,))],
        compiler_params=_cparams(("arbitrary",)),
        name="cross_sample",
    )(qc, mk, mv)


def _ca_out_body(ca_ref, x1_ref, wco_ref, g1_ref, g2_ref, x2_ref, h3_ref):
    x2 = x1_ref[...] + _rms(_dot(ca_ref[...], wco_ref[...]), g1_ref[...])
    x2_ref[...] = x2
    h3_ref[...] = _rms(x2, g2_ref[...]).astype(bf16)


def _ca_out(ca, x1, wco_bf, g1, g2, tm):
    t = x1.shape[0]
    row = pl.BlockSpec((tm, D_MODEL), lambda i: (i, 0))
    return pl.pallas_call(
        _ca_out_body,
        grid=(t // tm,),
        in_specs=[row, row, _const_spec(wco_bf.shape), _const_spec((1, D_MODEL)), _const_spec((1, D_MODEL))],
        out_specs=[row, row],
        out_shape=[jax.ShapeDtypeStruct((t, D_MODEL), f32), jax.ShapeDtypeStruct((t, D_MODEL), bf16)],
        compiler_params=_cparams(("arbitrary",)),
        name="ca_out",
    )(ca, x1, wco_bf, g1, g2)


FFN_CHUNK = 256


def _ffn_core(h3, x2, wup_ref, cw_ref, cb_ref, wdn_ref, g_ref, y_ref, act_scr, tm, *, tail=None, halo=None):
    if halo is not None:
        pos = lax.broadcasted_iota(jnp.int32, (tm, 1), 0) % halo[2]

    def conv_chunk(col):
        cs = slice(col, col + FFN_CHUNK)
        u = _dot(h3, wup_ref[:, cs])
        if tail is not None:
            tail_ref, ext_scr = tail
            ext_scr[8:8 + tm, cs] = u
            u1 = ext_scr[7:7 + tm, cs]
            u2 = ext_scr[6:6 + tm, cs]
            ext_scr[0:8, cs] = u[tm - 8:tm]
            tail_ref[:, cs] = u[tm - 8:tm]
        else:
            halo_ref, u_ref, _ = halo
            ext = jnp.concatenate([jnp.zeros((8, FFN_CHUNK), f32), u], axis=0)
            hal = halo_ref[:, cs]
            hext = jnp.concatenate([hal, jnp.zeros((8, FFN_CHUNK), f32)], axis=0)
            u1 = jnp.where(pos == 0, hext[1:1 + tm], ext[7:7 + tm])
            u2 = jnp.where(pos < 2, hal, ext[6:6 + tm])
            u_ref[:, cs] = u
        return cb_ref[:, cs] + cw_ref[2:3, cs] * u + cw_ref[1:2, cs] * u1 + cw_ref[0:1, cs] * u2

    for cj in range(D_FF // FFN_CHUNK):
        gate = conv_chunk(cj * FFN_CHUNK)
        val = conv_chunk(D_FF + cj * FFN_CHUNK)
        act_scr[:, cj * FFN_CHUNK:(cj + 1) * FFN_CHUNK] = (jax.nn.gelu(gate, approximate=True) * val).astype(bf16)
    y = _dot(act_scr[...], wdn_ref[...])
    y_ref[...] = x2 + _rms(y, g_ref[...])


def _ffn_sample_body(h3_ref, x2_ref, halo_ref, wup_ref, cw_ref, cb_ref, wdn_ref, g_ref, y_ref, u_ref, act_scr,
                     *, tm, dec):
    _ffn_core(h3_ref[...], x2_ref[...], wup_ref, cw_ref, cb_ref, wdn_ref, g_ref, y_ref, act_scr, tm,
              halo=(halo_ref, u_ref, dec))


def _post_ffn_body(x_ref, ohg_ref, osb_ref, wo_ref, g1_ref, g2_ref, wq_ref, mk_ref, mv_ref, wco_ref, g3_ref, g4_ref,
                   wup_ref, cw_ref, cb_ref, wdn_ref, g5_ref, y_ref, tail_ref, ext_scr, act_scr, *, tm):
    @pl.when(pl.program_id(1) == 0)
    def _():
        ext_scr[0:8, :] = jnp.zeros((8, ext_scr.shape[1]), f32)

    mixed = _dot(ohg_ref[...], wo_ref[0:HG_W, :]) + _dot(osb_ref[...], wo_ref[HG_W:HG_W + SB_W, :])
    x1 = x_ref[...] + _rms(mixed, g1_ref[...])
    qc = (_dot(_rms(x1, g2_ref[...]).astype(bf16), wq_ref[...]) * CA_SCALE).astype(bf16)
    mk = mk_ref[...].astype(bf16)
    mv = mv_ref[...].astype(bf16)
    ca = _cross_heads(qc, lambda h: mk[:, h * CA_DH:(h + 1) * CA_DH],
                      lambda h: mv[:, h * CA_DH:(h + 1) * CA_DH]).astype(bf16)
    x2 = x1 + _rms(_dot(ca, wco_ref[...]), g3_ref[...])
    h3 = _rms(x2, g4_ref[...]).astype(bf16)
    _ffn_core(h3, x2, wup_ref, cw_ref, cb_ref, wdn_ref, g5_ref, y_ref, act_scr, tm, tail=(tail_ref, ext_scr))


def _post_ffn(x, ohg, osb, wo_bf, g1, g2, wq_bf, mk, mv, wco_bf, g3, g4, wup_bf, cw, cb, wdn_bf, g5,
              n_seq, seq_len, tm):
    nblk = seq_len // tm
    row = lambda n: pl.BlockSpec((tm, n), lambda b, j: (b * nblk + j, 0))
    mem = pl.BlockSpec((N_MEM, D_MODEL), lambda b, j: (b, 0))
    gain = _const_spec((1, D_MODEL))
    return pl.pallas_call(
        functools.partial(_post_ffn_body, tm=tm),
        grid=(n_seq, nblk),
        in_specs=[row(D_MODEL), row(HG_W), row(SB_W), _const_spec(wo_bf.shape), gain, gain,
                  _const_spec(wq_bf.shape), mem, mem, _const_spec(wco_bf.shape), gain, gain,
                  _const_spec(wup_bf.shape), _const_spec(cw.shape), _const_spec(cb.shape),
                  _const_spec(wdn_bf.shape), gain],
        out_specs=[row(D_MODEL), pl.BlockSpec((8, 2 * D_FF), lambda b, j: (b, 0))],
        out_shape=[jax.ShapeDtypeStruct(x.shape, f32), jax.ShapeDtypeStruct((n_seq * 8, 2 * D_FF), f32)],
        scratch_shapes=[pltpu.VMEM((tm + 8, 2 * D_FF), f32), pltpu.VMEM((tm, D_FF), bf16)],
        compiler_params=_cparams(("arbitrary", "arbitrary")),
        name="post_ffn",
    )(x, ohg, osb, wo_bf, g1, g2, wq_bf, mk, mv, wco_bf, g3, g4, wup_bf, cw, cb, wdn_bf, g5)


def _ffn_sample(h3, x2, halo, wup_bf, cw, cb, wdn_bf, g, dec, tm):
    t = h3.shape[0]
    row = pl.BlockSpec((tm, D_MODEL), lambda i: (i, 0))
    wide = pl.BlockSpec((tm, 2 * D_FF), lambda i: (i, 0))
    return pl.pallas_call(
        functools.partial(_ffn_sample_body, tm=tm, dec=dec),
        grid=(t // tm,),
        in_specs=[row, row, wide, _const_spec(wup_bf.shape), _const_spec(cw.shape), _const_spec(cb.shape),
                  _const_spec(wdn_bf.shape), _const_spec((1, D_MODEL))],
        out_specs=[row, wide],
        out_shape=[jax.ShapeDtypeStruct((t, D_MODEL), f32), jax.ShapeDtypeStruct((t, 2 * D_FF), f32)],
        scratch_shapes=[pltpu.VMEM((tm, D_FF), bf16)],
        compiler_params=_cparams(("arbitrary",)),
        name="ffn_sample",
    )(h3, x2, halo, wup_bf, cw, cb, wdn_bf, g)


def kernel(x_prompt, x_sample, cache_sb_k, cache_sb_v, state_hgrn, state_ffn_conv, cache_mem_k, cache_mem_v,
           page_table, mem_prompt, w_in, hg_norm, hg_lb, sb_bias, w_o, g_mix_pre, g_mix_post, g_ca_pre, g_ca_post,
           g_mem, w_cq, w_ck, w_cv, w_co, g_ffn_pre, g_ffn_post, w_up, conv_w, conv_b, w_down):
    n_p, seq_len, _ = x_prompt.shape
    n_d, dec, _ = x_sample.shape
    depth = w_in.shape[0]
    assert depth == 1, "single-layer step"
    assert dec >= 2 and dec % 8 == 0, "the conv tail is taken from the new rows"
    l = 0
    row = lambda a: a[l].reshape(1, -1)
    w_in_bf, w_o_bf, w_cq_bf, w_co_bf = (w[l].astype(bf16) for w in (w_in, w_o, w_cq, w_co))
    w_ck_bf, w_cv_bf, w_up_bf, w_dn_bf = (w[l].astype(bf16) for w in (w_ck, w_cv, w_up, w_down))
    lbp = hg_lb[l:l + 2]
    gn = row(hg_norm)
    bias = sb_bias[l]
    cb = row(conv_b)
    cw = conv_w[l]

    xp = x_prompt.reshape(n_p * seq_len, D_MODEL)
    xs = x_sample.reshape(n_d * dec, D_MODEL)

    hin, sk_t, sv_t, qn, kb, vb = _in_proj(xp, row(g_mix_pre), w_in_bf, 512, n_seq=n_p)
    ohg, s_p = _hgrn_prompt(hin, lbp, gn, n_p, seq_len, 256, 16)
    osb = _sb_prompt(qn, kb, vb, bias, n_p, seq_len, 256)
    mk_p, mv_p = _mem_kv(mem_prompt.reshape(n_p * N_MEM, D_MODEL), row(g_mem), w_ck_bf, w_cv_bf, 256)
    yp, tail_p = _post_ffn(xp, ohg, osb, w_o_bf, row(g_mix_post), row(g_ca_pre), w_cq_bf, mk_p, mv_p, w_co_bf,
                           row(g_ca_post), row(g_ffn_pre), w_up_bf, cw, cb, w_dn_bf, row(g_ffn_post),
                           n_p, seq_len, 512)

    ts = n_d * dec
    hin_s, sk_s, sv_s, qn_s, kb_s, vb_s = _in_proj(xs, row(g_mix_pre), w_in_bf, 512)
    ohg_s, s_s = _hgrn_sample(hin_s, lbp, gn, state_hgrn[l], dec, 16)
    n_pool = cache_sb_k.shape[1]
    pages_t = lambda c: jnp.transpose(c[l], (0, 2, 3, 1)).reshape(n_pool, SB_W, PAGE)
    osb_s = _sb_sample(qn_s, kb_s, vb_s, pages_t(cache_sb_k), pages_t(cache_sb_v), page_table, bias,
                       dec).reshape(ts, SB_W)
    x1_s, qc_s = _mix_out(xs, ohg_s, osb_s, w_o_bf, row(g_mix_post), row(g_ca_pre), w_cq_bf, 512)
    ca_s = _cross_sample(qc_s, cache_mem_k[l], cache_mem_v[l], dec, 4)
    x2_s, h3_s = _ca_out(ca_s, x1_s, w_co_bf, row(g_ca_post), row(g_ffn_pre), 512)
    halo = jnp.pad(state_ffn_conv[l], ((0, 0), (0, dec - 2), (0, 0))).reshape(ts, 2 * D_FF)
    ys, u_s = _ffn_sample(h3_s, x2_s, halo, w_up_bf, cw, cb, w_dn_bf, row(g_ffn_post), dec, 128)

    kv_out = lambda a: jnp.transpose(a.reshape(n_p, SB_HEADS, SB_DH, seq_len), (0, 3, 1, 2))[None]
    return (yp.reshape(n_p, seq_len, D_MODEL), ys.reshape(n_d, dec, D_MODEL),
            kv_out(sk_t), kv_out(sv_t),
            s_p[None],
            tail_p.reshape(n_p, 8, 2 * D_FF)[None, :, 6:8],
            mk_p.reshape(1, n_p, N_MEM, CA_HEADS, CA_DH), mv_p.reshape(1, n_p, N_MEM, CA_HEADS, CA_DH),
            sk_s.reshape(1, n_d, dec, SB_HEADS, SB_DH), sv_s.reshape(1, n_d, dec, SB_HEADS, SB_DH),
            s_s[None],
            u_s.reshape(n_d, dec, 2 * D_FF)[None, :, dec - 2:dec])
```
